```python
import math
import jax, jax.numpy as jnp
from jax import lax
import numpy as np

D_MODEL = 1024
BATCH = 4
SEQ = 8192
DEPTH = 2
DEC_BATCH = 4
DEC_SEQ = 4096
PAST_LEN = 128

EPS = 1e-6
RG_WIDTH = D_MODEL // 2
RG_BLOCKS = 8
RG_BLOCK_DIM = RG_WIDTH // RG_BLOCKS
RG_CONV_W = 4
RG_C = 8.0
DA_HEADS = 8
DA_HEAD_DIM = 64
DA_WIDTH = DA_HEADS * DA_HEAD_DIM
DILATED_PATTERNS = ((128, 1), (512, 4), (2048, 16))
DA_HALF_STEPS = 64
DA_BLOCK = 64
AB_IN_COLS = 2 * RG_WIDTH + 3 * DA_WIDTH
MLA_HEADS = 16
MLA_Q_RANK = 384
MLA_KV_RANK = 256
MLA_NOPE = 64
MLA_ROPE = 32
MLA_V = 64
MLA_QK = MLA_NOPE + MLA_ROPE
MLA_IN_COLS = MLA_Q_RANK + MLA_KV_RANK + MLA_ROPE
MLA_Q_BLOCK = 128
ROPE_THETA = 10000.0
FFN_HIDDEN = int(math.ceil(8 * D_MODEL / 3 / 256) * 256)
NEG_BIG = -1e30
N_AB = (DEPTH + 1) // 2
N_C = DEPTH // 2

kernel_name = "hybrid_rglru_dilated_mla_encoder"


def _rmsnorm(x, g):
    xf = x.astype(jnp.float32)
    y = xf * lax.rsqrt(jnp.mean(xf * xf, axis=-1, keepdims=True) + EPS)
    return (y * g.astype(jnp.float32)).astype(x.dtype)


def _swiglu(h, w_gate, w_up, w_down):
    return (jax.nn.silu(h @ w_gate) * (h @ w_up)) @ w_down


def _depthwise_conv(x, w):
    C = x.shape[-1]
    return lax.conv_general_dilated(
        x, w[:, None, :], window_strides=(1,), padding=[(2, 1)],
        dimension_numbers=("NWC", "WIO", "NWC"), feature_group_count=C)


def _lin_combine(c1, c2):
    a1, b1 = c1
    a2, b2 = c2
    return a1 * a2, a2 * b1 + b2


def _rglru_direction(x, w_a, b_a, w_i, b_i, lam, reverse):
    B, S, C = x.shape
    xb = x.reshape(B, S, RG_BLOCKS, RG_BLOCK_DIM)
    r = jax.nn.sigmoid(jnp.einsum("bshi,hij->bshj", xb, w_a).reshape(B, S, C) + b_a)
    gi = jax.nn.sigmoid(jnp.einsum("bshi,hij->bshj", xb, w_i).reshape(B, S, C) + b_i)
    log_a = -RG_C * r * jax.nn.softplus(-lam.astype(jnp.float32))
    a = jnp.exp(log_a)
    u = jnp.sqrt(-jnp.expm1(2.0 * log_a)) * (gi * x)
    _, h = lax.associative_scan(_lin_combine, (a, u), reverse=reverse, axis=1)
    return h


def _dilated_branch(q, k, v, dil, slopes):
    B, S, H, Dh = q.shape
    L = S // dil
    nb = -(-L // DA_BLOCK)
    Lp = nb * DA_BLOCK

    def to_res(t):
        return t.reshape(B, L, dil, H, Dh).transpose(0, 2, 1, 3, 4)

    qr = jnp.pad(to_res(q), ((0, 0), (0, 0), (0, Lp - L), (0, 0), (0, 0)))
    qr = qr.reshape(B, dil, nb, DA_BLOCK, H, Dh)

    def windows(t):
        tp = jnp.pad(to_res(t), ((0, 0), (0, 0), (DA_BLOCK, Lp - L + DA_BLOCK), (0, 0), (0, 0)))
        tp = tp.reshape(B, dil, nb + 2, DA_BLOCK, H, Dh)
        return jnp.concatenate([tp[:, :, :-2], tp[:, :, 1:-1], tp[:, :, 2:]], axis=3)

    kw = windows(k)
    vw = windows(v)
    s = jnp.einsum("bdnqhc,bdnkhc->bdnhqk", qr, kw,
                   preferred_element_type=jnp.float32) * (1.0 / math.sqrt(Dh))
    qi = jnp.arange(nb)[:, None] * DA_BLOCK + jnp.arange(DA_BLOCK)[None, :]
    kj = jnp.arange(nb)[:, None] * DA_BLOCK - DA_BLOCK + jnp.arange(3 * DA_BLOCK)[None, :]
    rel = kj[:, None, :] - qi[:, :, None]
    valid = (jnp.abs(rel) <= DA_HALF_STEPS) & (kj[:, None, :] >= 0) & (kj[:, None, :] < L)
    dist = (dil * jnp.abs(rel)).astype(jnp.float32)
    bias = -slopes[None, :, None, None] * dist[:, None, :, :]
    s = jnp.where(valid[:, None, :, :], s + bias, NEG_BIG)
    lse = jax.nn.logsumexp(s, axis=-1)
    p = jnp.exp(s - lse[..., None])
    o = jnp.einsum("bdnhqk,bdnkhc->bdnqhc", p, vw.astype(jnp.float32))
    o = o.reshape(B, dil, Lp, H, Dh)[:, :, :L].transpose(0, 2, 1, 3, 4).reshape(B, S, H, Dh)
    lse = lse.transpose(0, 1, 2, 4, 3).reshape(B, dil, Lp, H)[:, :, :L]
    lse = lse.transpose(0, 2, 1, 3).reshape(B, S, H)
    return o, lse


def _alibi_slopes(n):
    return jnp.asarray([2.0 ** (-8.0 * (h + 1) / n) for h in range(n)], dtype=jnp.float32)


def _mixer_ab(h, w_in, conv_w, conv_b, w_a, b_a, w_i, b_i, lam, w_out):
    B, S, _ = h.shape
    proj = h @ w_in
    xr, gate, q, k, v = jnp.split(proj, 5, axis=-1)
    xr = (_depthwise_conv(xr, conv_w) + conv_b).astype(jnp.float32)
    hr = (_rglru_direction(xr, w_a[0], b_a[0], w_i[0], b_i[0], lam[0], False)
          + _rglru_direction(xr, w_a[1], b_a[1], w_i[1], b_i[1], lam[1], True))
    y_rnn = jax.nn.gelu(gate.astype(jnp.float32)) * hr
    q = q.reshape(B, S, DA_HEADS, DA_HEAD_DIM)
    k = k.reshape(B, S, DA_HEADS, DA_HEAD_DIM)
    v = v.reshape(B, S, DA_HEADS, DA_HEAD_DIM)
    slopes = _alibi_slopes(DA_HEADS)
    outs, lses = [], []
    for _, dil in DILATED_PATTERNS:
        o_g, lse_g = _dilated_branch(q, k, v, dil, slopes)
        outs.append(o_g)
        lses.append(lse_g)
    wgt = jax.nn.softmax(jnp.stack(lses, axis=0), axis=0)
    o = jnp.sum(wgt[..., None] * jnp.stack(outs, axis=0), axis=0)
    y = jnp.concatenate([y_rnn, o.reshape(B, S, DA_WIDTH)], axis=-1).astype(h.dtype)
    return y @ w_out


def _rope_tables(S):
    inv_freq = 1.0 / (ROPE_THETA ** (jnp.arange(0, MLA_ROPE, 2, dtype=jnp.float32) / MLA_ROPE))
    ang = jnp.arange(S, dtype=jnp.float32)[:, None] * inv_freq[None, :]
    return jnp.cos(ang), jnp.sin(ang)


def _rope(t, cos, sin):
    t = t.astype(jnp.float32)
    t1, t2 = jnp.split(t, 2, axis=-1)
    c = cos[None, :, None, :]
    s = sin[None, :, None, :]
    return jnp.concatenate([t1 * c - t2 * s, t1 * s + t2 * c], axis=-1)


def _mixer_mla(h, w_in, q_norm, w_qb, kv_norm, w_kvb, w_out, cos, sin):
    B, S, _ = h.shape
    proj = h @ w_in
    cq = proj[..., :MLA_Q_RANK]
    ckv = proj[..., MLA_Q_RANK:MLA_Q_RANK + MLA_KV_RANK]
    k_rope = proj[..., MLA_Q_RANK + MLA_KV_RANK:]
    qh = (_rmsnorm(cq, q_norm) @ w_qb).reshape(B, S, MLA_HEADS, MLA_QK)
    kvh = (_rmsnorm(ckv, kv_norm) @ w_kvb).reshape(B, S, MLA_HEADS, MLA_NOPE + MLA_V)
    q = jnp.concatenate([qh[..., :MLA_NOPE].astype(jnp.float32),
                         _rope(qh[..., MLA_NOPE:], cos, sin)], axis=-1)
    kr = _rope(k_rope[:, :, None, :], cos, sin)
    k = jnp.concatenate([kvh[..., :MLA_NOPE].astype(jnp.float32),
                         jnp.broadcast_to(kr, (B, S, MLA_HEADS, MLA_ROPE))], axis=-1)
    v = kvh[..., MLA_NOPE:].astype(jnp.float32)
    scale = 1.0 / math.sqrt(MLA_QK)
    nq = S // MLA_Q_BLOCK
    qb = q.reshape(B, nq, MLA_Q_BLOCK, MLA_HEADS, MLA_QK).transpose(1, 0, 2, 3, 4)

    def attend(qblk):
        s = jnp.einsum("bqhc,bkhc->bhqk", qblk, k) * scale
        p = jax.nn.softmax(s, axis=-1)
        return jnp.einsum("bhqk,bkhc->bqhc", p, v)

    o = lax.map(attend, qb)
    o = o.transpose(1, 0, 2, 3, 4).reshape(B, S, MLA_HEADS * MLA_V).astype(h.dtype)
    return o @ w_out


def _trunk(x, norm_mix, norm_ffn, norm_final,
           ab_w_in, ab_conv_w, ab_conv_b, rg_w_a, rg_b_a, rg_w_i, rg_b_i, rg_lam, ab_w_out,
           mla_w_in, mla_q_norm, mla_w_qb, mla_kv_norm, mla_w_kvb, mla_w_out,
           ffn_w_gate, ffn_w_up, ffn_w_down):
    S = x.shape[1]
    cos, sin = _rope_tables(S)
    for layer in range(DEPTH):
        h = _rmsnorm(x, norm_mix[layer])
        j = layer // 2
        if layer % 2 == 0:
            y = _mixer_ab(h, ab_w_in[j], ab_conv_w[j], ab_conv_b[j], rg_w_a[j], rg_b_a[j],
                          rg_w_i[j], rg_b_i[j], rg_lam[j], ab_w_out[j])
        else:
            y = _mixer_mla(h, mla_w_in[j], mla_q_norm[j], mla_w_qb[j], mla_kv_norm[j],
                           mla_w_kvb[j], mla_w_out[j], cos, sin)
        x = x + y
        h = _rmsnorm(x, norm_ffn[layer])
        x = x + _swiglu(h, ffn_w_gate[layer], ffn_w_up[layer], ffn_w_down[layer])
    return _rmsnorm(x, norm_final)


def setup_inputs(seed: int = 0) -> dict:
    key = jax.random.key(seed)
    ks = jax.random.split(key, 32)
    f32 = jnp.float32

    def w(k, shape, fan_in, gain=1.0):
        return jax.random.normal(k, shape, f32) * (gain * fan_in ** -0.5)

    def gain(k, shape):
        return 1.0 + 0.02 * jax.random.normal(k, shape, f32)

    def bias(k, shape):
        return 0.02 * jax.random.normal(k, shape, f32)

    a0 = jax.random.uniform(ks[12], (N_AB, 2, RG_WIDTH), f32, 0.9, 0.999)
    return {
        "x_prompt": jax.random.normal(ks[0], (BATCH, SEQ, D_MODEL), f32),
        "x_sample": jax.random.normal(ks[1], (DEC_BATCH, DEC_SEQ, D_MODEL), f32),
        "norm_mix": gain(ks[2], (DEPTH, D_MODEL)),
        "norm_ffn": gain(ks[3], (DEPTH, D_MODEL)),
        "norm_final": gain(ks[4], (D_MODEL,)),
        "ab_w_in": w(ks[5], (N_AB, D_MODEL, AB_IN_COLS), D_MODEL),
        "ab_conv_w": w(ks[6], (N_AB, RG_CONV_W, RG_WIDTH), RG_CONV_W),
        "ab_conv_b": bias(ks[7], (N_AB, RG_WIDTH)),
        "rg_w_a": w(ks[8], (N_AB, 2, RG_BLOCKS, RG_BLOCK_DIM, RG_BLOCK_DIM), RG_BLOCK_DIM),
        "rg_b_a": bias(ks[9], (N_AB, 2, RG_WIDTH)),
        "rg_w_i": w(ks[10], (N_AB, 2, RG_BLOCKS, RG_BLOCK_DIM, RG_BLOCK_DIM), RG_BLOCK_DIM),
        "rg_b_i": bias(ks[11], (N_AB, 2, RG_WIDTH)),
        "rg_lam": jnp.log(a0) - jnp.log1p(-a0),
        "ab_w_out": w(ks[13], (N_AB, RG_WIDTH + DA_WIDTH, D_MODEL), RG_WIDTH + DA_WIDTH, 0.5),
        "mla_w_in": w(ks[14], (N_C, D_MODEL, MLA_IN_COLS), D_MODEL),
        "mla_q_norm": gain(ks[15], (N_C, MLA_Q_RANK)),
        "mla_w_qb": w(ks[16], (N_C, MLA_Q_RANK, MLA_HEADS * MLA_QK), MLA_Q_RANK),
        "mla_kv_norm": gain(ks[17], (N_C, MLA_KV_RANK)),
        "mla_w_kvb": w(ks[18], (N_C, MLA_KV_RANK, MLA_HEADS * (MLA_NOPE + MLA_V)), MLA_KV_RANK),
        "mla_w_out": w(ks[19], (N_C, MLA_HEADS * MLA_V, D_MODEL), MLA_HEADS * MLA_V, 0.5),
        "ffn_w_gate": w(ks[20], (DEPTH, D_MODEL, FFN_HIDDEN), D_MODEL),
        "ffn_w_up": w(ks[21], (DEPTH, D_MODEL, FFN_HIDDEN), D_MODEL),
        "ffn_w_down": w(ks[22], (DEPTH, FFN_HIDDEN, D_MODEL), FFN_HIDDEN, 0.5),
    }


def reference(x_prompt, x_sample, norm_mix, norm_ffn, norm_final,
              ab_w_in, ab_conv_w, ab_conv_b, rg_w_a, rg_b_a, rg_w_i, rg_b_i, rg_lam, ab_w_out,
              mla_w_in, mla_q_norm, mla_w_qb, mla_kv_norm, mla_w_kvb, mla_w_out,
              ffn_w_gate, ffn_w_up, ffn_w_down):
    params = (norm_mix, norm_ffn, norm_final,
              ab_w_in, ab_conv_w, ab_conv_b, rg_w_a, rg_b_a, rg_w_i, rg_b_i, rg_lam, ab_w_out,
              mla_w_in, mla_q_norm, mla_w_qb, mla_kv_norm, mla_w_kvb, mla_w_out,
              ffn_w_gate, ffn_w_up, ffn_w_down)
    y_prompt = _trunk(x_prompt, *params)
    y_sample = _trunk(x_sample, *params)
    return (y_prompt, y_sample)
```

```python
import functools
import math

import jax
import jax.numpy as jnp
from jax import lax
from jax.experimental import pallas as pl
from jax.experimental.pallas import tpu as pltpu

F32 = jnp.float32
BF16 = jnp.bfloat16

D_MODEL = 1024
EPS = 1e-6
RG_WIDTH = 512
RG_BLOCKS = 8
RG_C = 8.0
DA_HEADS = 8
DA_HEAD_DIM = 64
DA_WIDTH = 512
DA_DILATIONS = (1, 4, 16)
DA_HALF_STEPS = 64
DA_TOKENS = DA_HALF_STEPS * max(DA_DILATIONS)
MLA_HEADS = 16
MLA_Q_RANK = 384
MLA_KV_RANK = 256
MLA_NOPE = 64
MLA_ROPE = 32
MLA_V = 64
MLA_QK = MLA_NOPE + MLA_ROPE
ROPE_THETA = 10000.0
FFN_HIDDEN = 2816
NEG_BIG = -1e30
LANES = 128

VMEM_LIMIT = 56 * 1024 * 1024

ROW_TILE = 512
SCAN_TILE = 512
MLA_Q_TILE = 256
MLA_K_TILE = ROW_TILE


def _params(*sem):
    return pltpu.CompilerParams(dimension_semantics=sem, vmem_limit_bytes=VMEM_LIMIT)


def _const_spec(shape):
    nd = len(shape)
    return pl.BlockSpec(shape, lambda *_: (0,) * nd)


def _rms(x, g):
    return x * lax.rsqrt(jnp.mean(x * x, axis=-1, keepdims=True) + EPS) * g


def _ab_in_kernel(x_ref, g_ref, w_ref, xg_ref, qkv_ref):
    h = _rms(x_ref[...], g_ref[...]).astype(BF16)
    xg_ref[...] = jnp.dot(h, w_ref[:, :2 * RG_WIDTH], preferred_element_type=F32)
    n_slabs = 3 * DA_WIDTH // LANES
    qkv = jnp.dot(h, w_ref[:, 2 * RG_WIDTH:], preferred_element_type=F32)
    for j in range(n_slabs):
        qkv_ref[j] = qkv[:, LANES * j:LANES * (j + 1)]


def _ab_in_proj(x, g, w):
    B, S, D = x.shape
    n_slabs = 3 * DA_WIDTH // LANES
    tm = ROW_TILE
    return pl.pallas_call(
        _ab_in_kernel,
        grid=(B, S // tm),
        in_specs=[
            pl.BlockSpec((None, tm, D), lambda b, i: (b, i, 0)),
            _const_spec(g.shape),
            _const_spec(w.shape),
        ],
        out_specs=[
            pl.BlockSpec((None, tm, 2 * RG_WIDTH), lambda b, i: (b, i, 0)),
            pl.BlockSpec((None, n_slabs, tm, LANES), lambda b, i: (b, 0, i, 0)),
        ],
        out_shape=[
            jax.ShapeDtypeStruct((B, S, 2 * RG_WIDTH), F32),
            jax.ShapeDtypeStruct((B, n_slabs, S, LANES), F32),
        ],
        compiler_params=_params("parallel", "parallel"),
        name="ab_in_proj",
    )(x, g, w)


def _rglru_kernel(reverse, n_chunks, *refs):
    if reverse:
        (x_ref, xp_ref, xn_ref, gate_ref, hf_ref, cw_ref, cb_ref, wg_ref, bg_ref, lam_ref,
         out_ref, carry_ref) = refs
    else:
        (x_ref, xp_ref, xn_ref, cw_ref, cb_ref, wg_ref, bg_ref, lam_ref,
         out_ref, carry_ref) = refs
    i = pl.program_id(1)
    c = (n_chunks - 1 - i) if reverse else i
    tc, width = x_ref.shape

    @pl.when(i == 0)
    def _():
        carry_ref[...] = jnp.zeros_like(carry_ref)

    x = x_ref[...]
    row = lax.broadcasted_iota(jnp.int32, (tc, width), 0)
    prev = jnp.where(c > 0, xp_ref[...], 0.0)
    nxt = jnp.where(c < n_chunks - 1, xn_ref[...], 0.0)
    p7, p6, n0 = prev[7:8], prev[6:7], nxt[0:1]
    xm1 = jnp.where(row == 0, p7, pltpu.roll(x, 1, 0))
    xm2 = jnp.where(row == 0, p6, jnp.where(row == 1, p7, pltpu.roll(x, 2, 0)))
    xp1 = jnp.where(row == tc - 1, n0, pltpu.roll(x, tc - 1, 0))
    cw = cw_ref[...]
    xc = cw[0:1] * xm2 + cw[1:2] * xm1 + cw[2:3] * x + cw[3:4] * xp1 + cb_ref[...]

    z = jnp.dot(xc.astype(BF16), wg_ref[...], preferred_element_type=F32) + bg_ref[...]
    r = jax.nn.sigmoid(z[:, :width])
    gi = jax.nn.sigmoid(z[:, width:])
    log_a = (-RG_C) * r * jax.nn.softplus(-lam_ref[...])
    a = jnp.exp(log_a)
    u = jnp.sqrt(-jnp.tanh(log_a) * (a * a + 1.0)) * (gi * xc)

    d = 1
    while d < tc:
        shift = (tc - d) if reverse else d
        valid = (row < tc - d) if reverse else (row >= d)
        a_sh = pltpu.roll(a, shift, 0)
        u_sh = pltpu.roll(u, shift, 0)
        u = jnp.where(valid, a * u_sh + u, u)
        a = jnp.where(valid, a * a_sh, a)
        d *= 2
    h = u + a * carry_ref[...]
    carry_ref[...] = h[0:1] if reverse else h[tc - 1:tc]

    if reverse:
        out_ref[...] = (jax.nn.gelu(gate_ref[...]) * (hf_ref[...] + h)).astype(out_ref.dtype)
    else:
        out_ref[...] = h


def _rglru_pass(xg, hf, conv_w, conv_b, wg, bg, lam, reverse):
    B, S, _ = xg.shape
    tc = SCAN_TILE
    n_chunks = S // tc
    sub = 8

    def chunk(i):
        return (n_chunks - 1 - i) if reverse else i

    x_spec = pl.BlockSpec((None, tc, RG_WIDTH), lambda b, i: (b, chunk(i), 0))
    prev_spec = pl.BlockSpec(
        (None, sub, RG_WIDTH), lambda b, i: (b, jnp.maximum(chunk(i) * (tc // sub) - 1, 0), 0))
    next_spec = pl.BlockSpec(
        (None, sub, RG_WIDTH),
        lambda b, i: (b, jnp.minimum((chunk(i) + 1) * (tc // sub), S // sub - 1), 0))
    consts = [conv_w, conv_b, wg, bg, lam]
    const_specs = [_const_spec(t.shape) for t in consts]
    if reverse:
        gate_spec = pl.BlockSpec((None, tc, RG_WIDTH), lambda b, i: (b, chunk(i), 1))
        hf_spec = pl.BlockSpec((None, tc, RG_WIDTH), lambda b, i: (b, chunk(i), 0))
        inputs = [xg, xg, xg, xg, hf] + consts
        in_specs = [x_spec, prev_spec, next_spec, gate_spec, hf_spec] + const_specs
        out_dtype = BF16
    else:
        inputs = [xg, xg, xg] + consts
        in_specs = [x_spec, prev_spec, next_spec] + const_specs
        out_dtype = F32
    return pl.pallas_call(
        functools.partial(_rglru_kernel, reverse, n_chunks),
        grid=(B, n_chunks),
        in_specs=in_specs,
        out_specs=pl.BlockSpec((None, tc, RG_WIDTH), lambda b, i: (b, chunk(i), 0)),
        out_shape=jax.ShapeDtypeStruct((B, S, RG_WIDTH), out_dtype),
        scratch_shapes=[pltpu.VMEM((1, RG_WIDTH), F32)],
        compiler_params=_params("parallel", "arbitrary"),
        name="rglru_bwd" if reverse else "rglru_fwd",
    )(*inputs)


def _dilated_kernel(n_blocks, slopes, q_ref, kp_ref, kc_ref, kn_ref, vp_ref, vc_ref, vn_ref,
                    out_ref, qs_ref, ks_ref, vs_ref, o_scr, lse_scr):
    i = pl.program_id(1)
    T = DA_TOKENS
    W = DA_HALF_STEPS
    n_pairs = DA_WIDTH // LANES
    scale = 1.0 / math.sqrt(DA_HEAD_DIM)
    is_first = i == 0
    is_last = i == n_blocks - 1

    for g, dil in enumerate(DA_DILATIONS):
        n = T // dil
        bq = min(n, 128)
        bk = bq + 2 * W
        rowq = lax.broadcasted_iota(jnp.int32, (bq, bk), 0)
        colk = lax.broadcasted_iota(jnp.int32, (bq, bk), 1)
        rel = colk - W - rowq
        band = jnp.abs(rel) <= W
        neg_dist = -(jnp.abs(rel) * dil).astype(F32)
        lane = lax.broadcasted_iota(jnp.int32, (bq, LANES), 1)
        lo_lane = lane < DA_HEAD_DIM

        def class_body(r, carry, dil=dil, n=n, bq=bq, bk=bk, g=g, colk=colk, band=band,
                       neg_dist=neg_dist, lo_lane=lo_lane):
            for p in range(n_pairs):
                qs_ref[p, 0:n] = (q_ref[p, pl.ds(r, n, stride=dil), :] * scale).astype(BF16)
                for src_p, src_c, src_n, dst in ((kp_ref, kc_ref, kn_ref, ks_ref),
                                                 (vp_ref, vc_ref, vn_ref, vs_ref)):
                    dst[p, 0:W] = src_p[p, pl.ds(r + dil * (n - W), W, stride=dil), :].astype(BF16)
                    dst[p, W:W + n] = src_c[p, pl.ds(r, n, stride=dil), :].astype(BF16)
                    dst[p, W + n:2 * W + n] = src_n[p, pl.ds(r, W, stride=dil), :].astype(BF16)

            def qb_body(qb, carry2):
                q0 = pl.multiple_of(qb * bq, bq)
                pos = q0 + colk
                in_seq = jnp.logical_not(
                    jnp.logical_or(jnp.logical_and(is_first, pos < W),
                                   jnp.logical_and(is_last, pos >= n + W)))
                valid = jnp.logical_and(band, in_seq)
                for p in range(n_pairs):
                    qp = qs_ref[p, pl.ds(q0, bq), :]
                    kp = ks_ref[p, pl.ds(q0, bk), :]
                    vp = vs_ref[p, pl.ds(q0, bk), :]
                    o_h, lse_h = [], []
                    for hh in range(2):
                        head = 2 * p + hh
                        qh = jnp.where(lo_lane if hh == 0 else jnp.logical_not(lo_lane), qp,
                                       jnp.zeros_like(qp))
                        s = lax.dot_general(qh, kp, (((1,), (1,)), ((), ())),
                                            preferred_element_type=F32)
                        s = jnp.where(valid, s + slopes[head] * neg_dist, NEG_BIG)
                        m = jnp.max(s, axis=-1, keepdims=True)
                        e = jnp.exp(s - m)
                        l = jnp.sum(e, axis=-1, keepdims=True)
                        o = jnp.dot(e.astype(BF16), vp, preferred_element_type=F32)
                        o_h.append(o / l)
                        lse_h.append(m + jnp.log(l))
                    rows = pl.ds(r + dil * q0, bq, stride=dil)
                    o_scr[g, p, rows, :] = jnp.where(lo_lane, o_h[0], o_h[1])
                    lse_scr[g, p, rows, :] = jnp.where(lo_lane, lse_h[0], lse_h[1])
                return carry2

            lax.fori_loop(0, n // bq, qb_body, 0)
            return carry

        lax.fori_loop(0, dil, class_body, 0)

    rows_per_step = 256

    def merge_body(t, carry):
        t0 = pl.multiple_of(t * rows_per_step, rows_per_step)
        rows = pl.ds(t0, rows_per_step)
        for p in range(n_pairs):
            lse = [lse_scr[g, p, rows, :] for g in range(len(DA_DILATIONS))]
            mx = jnp.maximum(jnp.maximum(lse[0], lse[1]), lse[2])
            w = [jnp.exp(v - mx) for v in lse]
            num = sum(w[g] * o_scr[g, p, rows, :] for g in range(len(DA_DILATIONS)))
            out_ref[rows, LANES * p:LANES * (p + 1)] = (num / (w[0] + w[1] + w[2])).astype(
                out_ref.dtype)
        return carry

    lax.fori_loop(0, T // rows_per_step, merge_body, 0)


def _dilated_attention(qkv):
    B, _, S, _ = qkv.shape
    T = DA_TOKENS
    n_blocks = S // T
    n_pairs = DA_WIDTH // LANES
    slopes = tuple(2.0 ** (-8.0 * (h + 1) / DA_HEADS) for h in range(DA_HEADS))

    def spec(slab, shift):
        def index(b, i):
            return (b, slab, jnp.clip(i + shift, 0, n_blocks - 1), 0)
        return pl.BlockSpec((None, n_pairs, T, LANES), index)

    win = T + 2 * DA_HALF_STEPS
    return pl.pallas_call(
        functools.partial(_dilated_kernel, n_blocks, slopes),
        grid=(B, n_blocks),
        in_specs=[spec(0, 0), spec(1, -1), spec(1, 0), spec(1, 1),
                  spec(2, -1), spec(2, 0), spec(2, 1)],
        out_specs=pl.BlockSpec((None, T, DA_WIDTH), lambda b, i: (b, i, 0)),
        out_shape=jax.ShapeDtypeStruct((B, S, DA_WIDTH), BF16),
        scratch_shapes=[
            pltpu.VMEM((n_pairs, T, LANES), BF16),
            pltpu.VMEM((n_pairs, win, LANES), BF16),
            pltpu.VMEM((n_pairs, win, LANES), BF16),
            pltpu.VMEM((len(DA_DILATIONS), n_pairs, T, LANES), F32),
            pltpu.VMEM((len(DA_DILATIONS), n_pairs, T, LANES), F32),
        ],
        compiler_params=_params("parallel", "parallel"),
        name="dilated_attention",
    )(qkv, qkv, qkv, qkv, qkv, qkv, qkv)


def _out_ffn_kernel(n_y, final_norm, *refs):
    x_ref = refs[0]
    y_refs = refs[1:1 + n_y]
    wo_ref, g_ref, wg_ref, wu_ref, wd_ref = refs[1 + n_y:6 + n_y]
    gf_ref = refs[6 + n_y] if final_norm else None
    out_ref = refs[-1]

    y = y_refs[0][...] if n_y == 1 else jnp.concatenate([r[...] for r in y_refs], axis=1)
    x = x_ref[...] + jnp.dot(y, wo_ref[...], preferred_element_type=F32)
    h = _rms(x, g_ref[...]).astype(BF16)
    gate = jnp.dot(h, wg_ref[...], preferred_element_type=F32)
    up = jnp.dot(h, wu_ref[...], preferred_element_type=F32)
    act = (jax.nn.silu(gate) * up).astype(BF16)
    x = x + jnp.dot(act, wd_ref[...], preferred_element_type=F32)
    if final_norm:
        x = _rms(x, gf_ref[...])
    out_ref[...] = x


def _out_ffn(x, ys, w_out, g, wg, wu, wd, g_final=None):
    B, S, D = x.shape
    rows = B * S
    tm = ROW_TILE
    x2 = x.reshape(rows, D)
    ys2 = [y.reshape(rows, y.shape[-1]) for y in ys]
    consts = [w_out, g, wg, wu, wd] + ([g_final] if g_final is not None else [])
    out = pl.pallas_call(
        functools.partial(_out_ffn_kernel, len(ys), g_final is not None),
        grid=(rows // tm,),
        in_specs=([pl.BlockSpec((tm, D), lambda i: (i, 0))]
                  + [pl.BlockSpec((tm, y.shape[-1]), lambda i: (i, 0)) for y in ys2]
                  + [pl.BlockSpec(t.shape, lambda i, nd=t.ndim: (0,) * nd,
                                  pipeline_mode=pl.Buffered(1)) for t in consts]),
        out_specs=pl.BlockSpec((tm, D), lambda i: (i, 0)),
        out_shape=jax.ShapeDtypeStruct((rows, D), F32),
        compiler_params=_params("parallel"),
        name="out_proj_ffn",
    )(x2, *ys2, *consts)
    return out.reshape(B, S, D)


def _mla_in_kernel(x_ref, g_ref, win_ref, qn_ref, kvn_ref, wq_ref, wk_ref, wvt_ref, cos_ref,
                   sin_ref, q_ref, k_ref, vt_ref):
    h = _rms(x_ref[...], g_ref[...]).astype(BF16)
    proj = jnp.dot(h, win_ref[...], preferred_element_type=F32)
    cq = _rms(proj[:, :MLA_Q_RANK], qn_ref[...]).astype(BF16)
    ckv = _rms(proj[:, MLA_Q_RANK:MLA_Q_RANK + MLA_KV_RANK], kvn_ref[...]).astype(BF16)
    base = MLA_Q_RANK + MLA_KV_RANK
    cos = cos_ref[...]
    sin = sin_ref[...]
    k_rope = proj[:, base:base + LANES] * cos + proj[:, base + LANES:base + 2 * LANES] * sin
    scale = 1.0 / math.sqrt(MLA_QK)
    width = MLA_HEADS * LANES
    for hd in range(MLA_HEADS):
        cols = slice(hd * LANES, (hd + 1) * LANES)
        rot_cols = slice(width + hd * LANES, width + (hd + 1) * LANES)
        qa = jnp.dot(cq, wq_ref[:, cols], preferred_element_type=F32)
        qb = jnp.dot(cq, wq_ref[:, rot_cols], preferred_element_type=F32)
        q_ref[:, cols] = ((qa * cos + qb * sin) * scale).astype(BF16)
        kn = jnp.dot(ckv, wk_ref[:, cols], preferred_element_type=F32)
        k_ref[:, cols] = (kn + k_rope).astype(BF16)
    vt_ref[...] = lax.dot_general(wvt_ref[...], ckv, (((1,), (1,)), ((), ())),
                                  preferred_element_type=F32).astype(BF16)


def _mla_in_proj(x, g, win, qn, kvn, wq, wk, wvt, cos_tab, sin_tab):
    B, S, D = x.shape
    tm = ROW_TILE
    width = MLA_HEADS * LANES
    consts = [g, win, qn, kvn, wq, wk, wvt]
    return pl.pallas_call(
        _mla_in_kernel,
        grid=(B, S // tm),
        in_specs=([pl.BlockSpec((None, tm, D), lambda b, i: (b, i, 0))]
                  + [_const_spec(t.shape) for t in consts]
                  + [pl.BlockSpec((tm, LANES), lambda b, i: (i, 0)),
                     pl.BlockSpec((tm, LANES), lambda b, i: (i, 0))]),
        out_specs=[
            pl.BlockSpec((None, tm, width), lambda b, i: (b, i, 0)),
            pl.BlockSpec((None, tm, width), lambda b, i: (b, i, 0)),
            pl.BlockSpec((None, None, MLA_HEADS * MLA_V, tm), lambda b, i: (b, i, 0, 0)),
        ],
        out_shape=[
            jax.ShapeDtypeStruct((B, S, width), BF16),
            jax.ShapeDtypeStruct((B, S, width), BF16),
            jax.ShapeDtypeStruct((B, S // tm, MLA_HEADS * MLA_V, tm), BF16),
        ],
        compiler_params=_params("parallel", "parallel"),
        name="mla_in_proj",
    )(x, *consts, cos_tab, sin_tab)


def _mla_attn_kernel(q_ref, k_ref, vt_ref, o_ref):
    n_kb, _, bk = vt_ref.shape
    bq = q_ref.shape[0]
    outs = []
    for hh in range(2):
        qh = q_ref[:, hh * LANES:(hh + 1) * LANES]

        def body(j, carry, hh=hh, qh=qh):
            m, l, acc = carry
            k0 = pl.multiple_of(j * bk, bk)
            kb = k_ref[pl.ds(k0, bk), hh * LANES:(hh + 1) * LANES]
            st = lax.dot_general(kb, qh, (((1,), (1,)), ((), ())), preferred_element_type=F32)
            m_new = jnp.maximum(m, jnp.max(st, axis=0, keepdims=True))
            alpha = jnp.exp(m - m_new)
            p = jnp.exp(st - m_new)
            l = alpha * l + jnp.sum(p, axis=0, keepdims=True)
            vb = vt_ref[j, hh * MLA_V:(hh + 1) * MLA_V, :]
            acc = alpha * acc + jnp.dot(vb, p.astype(BF16), preferred_element_type=F32)
            return m_new, l, acc

        init = (jnp.full((1, bq), NEG_BIG, F32), jnp.zeros((1, bq), F32),
                jnp.zeros((MLA_V, bq), F32))
        _, l, acc = lax.fori_loop(0, n_kb, body, init)
        outs.append(acc / l)
    o_ref[...] = jnp.concatenate(outs, axis=0).T.astype(o_ref.dtype)


def _mla_attention(q, k, vt):
    B, S, _ = q.shape
    n_kb = vt.shape[1]
    bq = MLA_Q_TILE
    n_pairs = MLA_HEADS // 2
    return pl.pallas_call(
        _mla_attn_kernel,
        grid=(B, n_pairs, S // bq),
        in_specs=[
            pl.BlockSpec((None, bq, 2 * LANES), lambda b, p, i: (b, i, p)),
            pl.BlockSpec((None, S, 2 * LANES), lambda b, p, i: (b, 0, p)),
            pl.BlockSpec((None, n_kb, 2 * MLA_V, MLA_K_TILE), lambda b, p, i: (b, 0, p, 0)),
        ],
        out_specs=pl.BlockSpec((None, bq, 2 * MLA_V), lambda b, p, i: (b, i, p)),
        out_shape=jax.ShapeDtypeStruct((B, S, MLA_HEADS * MLA_V), BF16),
        compiler_params=_params("parallel", "parallel", "arbitrary"),
        name="mla_attention",
    )(q, k, vt)


def _block_diag(w):
    nb, d, _ = w.shape
    eye = jnp.eye(nb, dtype=w.dtype)
    return (eye[:, None, :, None] * w[:, :, None, :]).reshape(nb * d, nb * d)


def _rot_cols(w):
    half = MLA_ROPE // 2
    return jnp.concatenate([-w[..., half:], w[..., :half]], axis=-1)


def _pad_lanes(w, offset):
    return jnp.pad(w, [(0, 0)] * (w.ndim - 1) + [(offset, LANES - offset - w.shape[-1])])


def _prepare(norm_mix, norm_ffn, norm_final, ab_w_in, ab_conv_w, ab_conv_b, rg_w_a, rg_b_a,
             rg_w_i, rg_b_i, rg_lam, ab_w_out, mla_w_in, mla_q_norm, mla_w_qb, mla_kv_norm,
             mla_w_kvb, mla_w_out, ffn_w_gate, ffn_w_up, ffn_w_down):
    row = lambda v: v.reshape(1, -1).astype(F32)
    P = {}
    P["norm_mix"] = [row(norm_mix[l]) for l in range(2)]
    P["norm_ffn"] = [row(norm_ffn[l]) for l in range(2)]
    P["norm_final"] = row(norm_final)
    P["ab_w_in"] = ab_w_in[0].astype(BF16)
    P["conv_w"] = ab_conv_w[0].astype(F32)
    P["conv_b"] = row(ab_conv_b[0])
    P["rg_w"] = [jnp.concatenate([_block_diag(rg_w_a[0, d]), _block_diag(rg_w_i[0, d])],
                                 axis=1).astype(BF16) for d in range(2)]
    P["rg_b"] = [jnp.concatenate([rg_b_a[0, d], rg_b_i[0, d]]).reshape(1, -1) for d in range(2)]
    P["rg_lam"] = [row(rg_lam[0, d]) for d in range(2)]
    P["ab_w_out"] = ab_w_out[0].astype(BF16)

    w_in = mla_w_in[0]
    base = MLA_Q_RANK + MLA_KV_RANK
    w_kr = w_in[:, base:]
    P["mla_w_in"] = jnp.concatenate(
        [w_in[:, :base], _pad_lanes(w_kr, MLA_NOPE), _pad_lanes(_rot_cols(w_kr), MLA_NOPE)],
        axis=1).astype(BF16)
    P["mla_q_norm"] = row(mla_q_norm[0])
    P["mla_kv_norm"] = row(mla_kv_norm[0])
    wq = mla_w_qb[0].reshape(MLA_Q_RANK, MLA_HEADS, MLA_QK)
    wq_plain = _pad_lanes(wq, 0)
    wq_rot = _pad_lanes(_rot_cols(wq[..., MLA_NOPE:]), MLA_NOPE)
    P["mla_wq"] = jnp.concatenate(
        [wq_plain.reshape(MLA_Q_RANK, -1), wq_rot.reshape(MLA_Q_RANK, -1)], axis=1).astype(BF16)
    wkv = mla_w_kvb[0].reshape(MLA_KV_RANK, MLA_HEADS, MLA_NOPE + MLA_V)
    P["mla_wk"] = _pad_lanes(wkv[..., :MLA_NOPE], 0).reshape(MLA_KV_RANK, -1).astype(BF16)
    P["mla_wvt"] = wkv[..., MLA_NOPE:].reshape(MLA_KV_RANK, -1).T.astype(BF16)
    P["mla_w_out"] = mla_w_out[0].astype(BF16)
    P["ffn"] = [(ffn_w_gate[l].astype(BF16), ffn_w_up[l].astype(BF16),
                 ffn_w_down[l].astype(BF16)) for l in range(2)]
    return P


def _rope_slabs(S):
    inv_freq = 1.0 / (ROPE_THETA ** (jnp.arange(0, MLA_ROPE, 2, dtype=F32) / MLA_ROPE))
    ang = jnp.arange(S, dtype=F32)[:, None] * inv_freq[None, :]
    cos, sin = jnp.cos(ang), jnp.sin(ang)
    pad = LANES - MLA_QK
    cos_tab = jnp.concatenate([jnp.ones((S, MLA_NOPE), F32), cos, cos, jnp.zeros((S, pad), F32)], 1)
    sin_tab = jnp.concatenate([jnp.zeros((S, MLA_NOPE), F32), sin, sin, jnp.zeros((S, pad), F32)], 1)
    return cos_tab, sin_tab


def _trunk(x, P):
    S = x.shape[1]
    xg, qkv = _ab_in_proj(x, P["norm_mix"][0], P["ab_w_in"])
    hf = _rglru_pass(xg, None, P["conv_w"], P["conv_b"], P["rg_w"][0], P["rg_b"][0],
                     P["rg_lam"][0], reverse=False)
    y_rnn = _rglru_pass(xg, hf, P["conv_w"], P["conv_b"], P["rg_w"][1], P["rg_b"][1],
                        P["rg_lam"][1], reverse=True)
    o = _dilated_attention(qkv)
    x = _out_ffn(x, [y_rnn, o], P["ab_w_out"], P["norm_ffn"][0], *P["ffn"][0])
    cos_tab, sin_tab = _rope_slabs(S)
    q, k, vt = _mla_in_proj(x, P["norm_mix"][1], P["mla_w_in"], P["mla_q_norm"], P["mla_kv_norm"],
                            P["mla_wq"], P["mla_wk"], P["mla_wvt"], cos_tab, sin_tab)
    o = _mla_attention(q, k, vt)
    return _out_ffn(x, [o], P["mla_w_out"], P["norm_ffn"][1], *P["ffn"][1],
                    g_final=P["norm_final"])


def kernel(x_prompt, x_sample, norm_mix, norm_ffn, norm_final, ab_w_in, ab_conv_w, ab_conv_b, rg_w_a, rg_b_a, rg_w_i, rg_b_i, rg_lam, ab_w_out, mla_w_in, mla_q_norm, mla_w_qb, mla_kv_norm, mla_w_kvb, mla_w_out, ffn_w_gate, ffn_w_up, ffn_w_down):
    P = _prepare(norm_mix, norm_ffn, norm_final, ab_w_in, ab_conv_w, ab_conv_b, rg_w_a, rg_b_a,
                 rg_w_i, rg_b_i, rg_lam, ab_w_out, mla_w_in, mla_q_norm, mla_w_qb, mla_kv_norm,
                 mla_w_kvb, mla_w_out, ffn_w_gate, ffn_w_up, ffn_w_down)
    return (_trunk(x_prompt, P), _trunk(x_sample, P))
```

```python
import functools
import math

import jax
import jax.numpy as jnp
from jax import lax
from jax.experimental import pallas as pl
from jax.experimental.pallas import tpu as pltpu

F32 = jnp.float32
BF16 = jnp.bfloat16

D_MODEL = 1024
EPS = 1e-6
RG_WIDTH = 512
RG_BLOCKS = 8
RG_C = 8.0
DA_HEADS = 8
DA_HEAD_DIM = 64
DA_WIDTH = 512
DA_DILATIONS = (1, 4, 16)
DA_HALF_STEPS = 64
DA_TOKENS = DA_HALF_STEPS * max(DA_DILATIONS)
MLA_HEADS = 16
MLA_Q_RANK = 384
MLA_KV_RANK = 256
MLA_NOPE = 64
MLA_ROPE = 32
MLA_V = 64
MLA_QK = MLA_NOPE + MLA_ROPE
ROPE_THETA = 10000.0
FFN_HIDDEN = 2816
NEG_BIG = -1e30
LANES = 128

VMEM_LIMIT = 56 * 1024 * 1024

ROW_TILE = 512
SCAN_TILE = 512
MLA_Q_TILE = 512
MLA_K_TILE = ROW_TILE


def _params(*sem):
    return pltpu.CompilerParams(dimension_semantics=sem, vmem_limit_bytes=VMEM_LIMIT)


def _const_spec(shape):
    nd = len(shape)
    return pl.BlockSpec(shape, lambda *_: (0,) * nd)


def _rms(x, g):
    return x * lax.rsqrt(jnp.mean(x * x, axis=-1, keepdims=True) + EPS) * g


def _ab_in_kernel(x_ref, g_ref, w_ref, xg_ref, qkv_ref):
    h = _rms(x_ref[...], g_ref[...]).astype(BF16)
    xg_ref[...] = jnp.dot(h, w_ref[:, :2 * RG_WIDTH], preferred_element_type=F32)
    n_slabs = 3 * DA_WIDTH // LANES
    qkv = jnp.dot(h, w_ref[:, 2 * RG_WIDTH:], preferred_element_type=F32)
    for j in range(n_slabs):
        qkv_ref[j] = qkv[:, LANES * j:LANES * (j + 1)]


def _ab_in_proj(x, g, w):
    B, S, D = x.shape
    n_slabs = 3 * DA_WIDTH // LANES
    tm = ROW_TILE
    return pl.pallas_call(
        _ab_in_kernel,
        grid=(B, S // tm),
        in_specs=[
            pl.BlockSpec((None, tm, D), lambda b, i: (b, i, 0)),
            _const_spec(g.shape),
            _const_spec(w.shape),
        ],
        out_specs=[
            pl.BlockSpec((None, tm, 2 * RG_WIDTH), lambda b, i: (b, i, 0)),
            pl.BlockSpec((None, n_slabs, tm, LANES), lambda b, i: (b, 0, i, 0)),
        ],
        out_shape=[
            jax.ShapeDtypeStruct((B, S, 2 * RG_WIDTH), F32),
            jax.ShapeDtypeStruct((B, n_slabs, S, LANES), F32),
        ],
        compiler_params=_params("parallel", "parallel"),
        name="ab_in_proj",
    )(x, g, w)


def _rglru_kernel(reverse, n_chunks, *refs):
    if reverse:
        (x_ref, xp_ref, xn_ref, gate_ref, hf_ref, cw_ref, cb_ref, wg_ref, bg_ref, lam_ref,
         out_ref, carry_ref) = refs
    else:
        (x_ref, xp_ref, xn_ref, cw_ref, cb_ref, wg_ref, bg_ref, lam_ref,
         out_ref, carry_ref) = refs
    i = pl.program_id(1)
    c = (n_chunks - 1 - i) if reverse else i
    tc, width = x_ref.shape

    @pl.when(i == 0)
    def _():
        carry_ref[...] = jnp.zeros_like(carry_ref)

    x = x_ref[...]
    row = lax.broadcasted_iota(jnp.int32, (tc, width), 0)
    prev = jnp.where(c > 0, xp_ref[...], 0.0)
    nxt = jnp.where(c < n_chunks - 1, xn_ref[...], 0.0)
    p7, p6, n0 = prev[7:8], prev[6:7], nxt[0:1]
    xm1 = jnp.where(row == 0, p7, pltpu.roll(x, 1, 0))
    xm2 = jnp.where(row == 0, p6, jnp.where(row == 1, p7, pltpu.roll(x, 2, 0)))
    xp1 = jnp.where(row == tc - 1, n0, pltpu.roll(x, tc - 1, 0))
    cw = cw_ref[...]
    xc = cw[0:1] * xm2 + cw[1:2] * xm1 + cw[2:3] * x + cw[3:4] * xp1 + cb_ref[...]

    z = jnp.dot(xc.astype(BF16), wg_ref[...], preferred_element_type=F32) + bg_ref[...]
    r = jax.nn.sigmoid(z[:, :width])
    gi = jax.nn.sigmoid(z[:, width:])
    log_a = (-RG_C) * r * jax.nn.softplus(-lam_ref[...])
    a = jnp.exp(log_a)
    u = jnp.sqrt(-jnp.tanh(log_a) * (a * a + 1.0)) * (gi * xc)

    d = 1
    while d < tc:
        shift = (tc - d) if reverse else d
        valid = (row < tc - d) if reverse else (row >= d)
        a_sh = pltpu.roll(a, shift, 0)
        u_sh = pltpu.roll(u, shift, 0)
        u = jnp.where(valid, a * u_sh + u, u)
        a = jnp.where(valid, a * a_sh, a)
        d *= 2
    h = u + a * carry_ref[...]
    carry_ref[...] = h[0:1] if reverse else h[tc - 1:tc]

    if reverse:
        out_ref[...] = (jax.nn.gelu(gate_ref[...]) * (hf_ref[...] + h)).astype(out_ref.dtype)
    else:
        out_ref[...] = h


def _rglru_pass(xg, hf, conv_w, conv_b, wg, bg, lam, reverse):
    B, S, _ = xg.shape
    tc = SCAN_TILE
    n_chunks = S // tc
    sub = 8

    def chunk(i):
        return (n_chunks - 1 - i) if reverse else i

    x_spec = pl.BlockSpec((None, tc, RG_WIDTH), lambda b, i: (b, chunk(i), 0))
    prev_spec = pl.BlockSpec(
        (None, sub, RG_WIDTH), lambda b, i: (b, jnp.maximum(chunk(i) * (tc // sub) - 1, 0), 0))
    next_spec = pl.BlockSpec(
        (None, sub, RG_WIDTH),
        lambda b, i: (b, jnp.minimum((chunk(i) + 1) * (tc // sub), S // sub - 1), 0))
    consts = [conv_w, conv_b, wg, bg, lam]
    const_specs = [_const_spec(t.shape) for t in consts]
    if reverse:
        gate_spec = pl.BlockSpec((None, tc, RG_WIDTH), lambda b, i: (b, chunk(i), 1))
        hf_spec = pl.BlockSpec((None, tc, RG_WIDTH), lambda b, i: (b, chunk(i), 0))
        inputs = [xg, xg, xg, xg, hf] + consts
        in_specs = [x_spec, prev_spec, next_spec, gate_spec, hf_spec] + const_specs
        out_dtype = BF16
    else:
        inputs = [xg, xg, xg] + consts
        in_specs = [x_spec, prev_spec, next_spec] + const_specs
        out_dtype = F32
    return pl.pallas_call(
        functools.partial(_rglru_kernel, reverse, n_chunks),
        grid=(B, n_chunks),
        in_specs=in_specs,
        out_specs=pl.BlockSpec((None, tc, RG_WIDTH), lambda b, i: (b, chunk(i), 0)),
        out_shape=jax.ShapeDtypeStruct((B, S, RG_WIDTH), out_dtype),
        scratch_shapes=[pltpu.VMEM((1, RG_WIDTH), F32)],
        compiler_params=_params("parallel", "arbitrary"),
        name="rglru_bwd" if reverse else "rglru_fwd",
    )(*inputs)


def _dilated_kernel(n_blocks, slopes, q_ref, kp_ref, kc_ref, kn_ref, vp_ref, vc_ref, vn_ref,
                    out_ref, qs_ref, ks_ref, vs_ref, o_scr, lse_scr):
    i = pl.program_id(1)
    T = DA_TOKENS
    W = DA_HALF_STEPS
    n_pairs = DA_WIDTH // LANES
    scale = 1.0 / math.sqrt(DA_HEAD_DIM)
    is_first = i == 0
    is_last = i == n_blocks - 1

    for g, dil in enumerate(DA_DILATIONS):
        n = T // dil
        bq = min(n, 128)
        bk = bq + 2 * W
        rowq = lax.broadcasted_iota(jnp.int32, (bq, bk), 0)
        colk = lax.broadcasted_iota(jnp.int32, (bq, bk), 1)
        rel = colk - W - rowq
        band = jnp.abs(rel) <= W
        neg_dist = -(jnp.abs(rel) * dil).astype(F32)
        lane = lax.broadcasted_iota(jnp.int32, (bq, LANES), 1)
        lo_lane = lane < DA_HEAD_DIM

        def class_body(r, carry, dil=dil, n=n, bq=bq, bk=bk, g=g, colk=colk, band=band,
                       neg_dist=neg_dist, lo_lane=lo_lane):
            for p in range(n_pairs):
                qs_ref[p, 0:n] = (q_ref[p, pl.ds(r, n, stride=dil), :] * scale).astype(BF16)
                for src_p, src_c, src_n, dst in ((kp_ref, kc_ref, kn_ref, ks_ref),
                                                 (vp_ref, vc_ref, vn_ref, vs_ref)):
                    dst[p, 0:W] = src_p[p, pl.ds(r + dil * (n - W), W, stride=dil), :].astype(BF16)
                    dst[p, W:W + n] = src_c[p, pl.ds(r, n, stride=dil), :].astype(BF16)
                    dst[p, W + n:2 * W + n] = src_n[p, pl.ds(r, W, stride=dil), :].astype(BF16)

            def qb_body(qb, carry2):
                q0 = pl.multiple_of(qb * bq, bq)
                pos = q0 + colk
                in_seq = jnp.logical_not(
                    jnp.logical_or(jnp.logical_and(is_first, pos < W),
                                   jnp.logical_and(is_last, pos >= n + W)))
                valid = jnp.logical_and(band, in_seq)
                for p in range(n_pairs):
                    qp = qs_ref[p, pl.ds(q0, bq), :]
                    kp = ks_ref[p, pl.ds(q0, bk), :]
                    vp = vs_ref[p, pl.ds(q0, bk), :]
                    o_h, lse_h = [], []
                    for hh in range(2):
                        head = 2 * p + hh
                        qh = jnp.where(lo_lane if hh == 0 else jnp.logical_not(lo_lane), qp,
                                       jnp.zeros_like(qp))
                        s = lax.dot_general(qh, kp, (((1,), (1,)), ((), ())),
                                            preferred_element_type=F32)
                        s = jnp.where(valid, s + slopes[head] * neg_dist, NEG_BIG)
                        m = jnp.max(s, axis=-1, keepdims=True)
                        e = jnp.exp(s - m)
                        l = jnp.sum(e, axis=-1, keepdims=True)
                        o = jnp.dot(e.astype(BF16), vp, preferred_element_type=F32)
                        o_h.append(o / l)
                        lse_h.append(m + jnp.log(l))
                    rows = pl.ds(r + dil * q0, bq, stride=dil)
                    o_scr[g, p, rows, :] = jnp.where(lo_lane, o_h[0], o_h[1])
                    lse_scr[g, p, rows, :] = jnp.where(lo_lane, lse_h[0], lse_h[1])
                return carry2

            lax.fori_loop(0, n // bq, qb_body, 0)
            return carry

        lax.fori_loop(0, dil, class_body, 0)

    rows_per_step = 256

    def merge_body(t, carry):
        t0 = pl.multiple_of(t * rows_per_step, rows_per_step)
        rows = pl.ds(t0, rows_per_step)
        for p in range(n_pairs):
            lse = [lse_scr[g, p, rows, :] for g in range(len(DA_DILATIONS))]
            mx = jnp.maximum(jnp.maximum(lse[0], lse[1]), lse[2])
            w = [jnp.exp(v - mx) for v in lse]
            num = sum(w[g] * o_scr[g, p, rows, :] for g in range(len(DA_DILATIONS)))
            out_ref[rows, LANES * p:LANES * (p + 1)] = (num / (w[0] + w[1] + w[2])).astype(
                out_ref.dtype)
        return carry

    lax.fori_loop(0, T // rows_per_step, merge_body, 0)


def _dilated_attention(qkv):
    B, _, S, _ = qkv.shape
    T = DA_TOKENS
    n_blocks = S // T
    n_pairs = DA_WIDTH // LANES
    slopes = tuple(2.0 ** (-8.0 * (h + 1) / DA_HEADS) for h in range(DA_HEADS))

    def spec(slab, shift):
        def index(b, i):
            return (b, slab, jnp.clip(i + shift, 0, n_blocks - 1), 0)
        return pl.BlockSpec((None, n_pairs, T, LANES), index)

    win = T + 2 * DA_HALF_STEPS
    return pl.pallas_call(
        functools.partial(_dilated_kernel, n_blocks, slopes),
        grid=(B, n_blocks),
        in_specs=[spec(0, 0), spec(1, -1), spec(1, 0), spec(1, 1),
                  spec(2, -1), spec(2, 0), spec(2, 1)],
        out_specs=pl.BlockSpec((None, T, DA_WIDTH), lambda b, i: (b, i, 0)),
        out_shape=jax.ShapeDtypeStruct((B, S, DA_WIDTH), BF16),
        scratch_shapes=[
            pltpu.VMEM((n_pairs, T, LANES), BF16),
            pltpu.VMEM((n_pairs, win, LANES), BF16),
            pltpu.VMEM((n_pairs, win, LANES), BF16),
            pltpu.VMEM((len(DA_DILATIONS), n_pairs, T, LANES), F32),
            pltpu.VMEM((len(DA_DILATIONS), n_pairs, T, LANES), F32),
        ],
        compiler_params=_params("parallel", "parallel"),
        name="dilated_attention",
    )(qkv, qkv, qkv, qkv, qkv, qkv, qkv)


def _out_ffn_kernel(n_y, final_norm, *refs):
    x_ref = refs[0]
    y_refs = refs[1:1 + n_y]
    wo_ref, g_ref, wg_ref, wu_ref, wd_ref = refs[1 + n_y:6 + n_y]
    gf_ref = refs[6 + n_y] if final_norm else None
    out_ref = refs[-1]

    y = y_refs[0][...] if n_y == 1 else jnp.concatenate([r[...] for r in y_refs], axis=1)
    x = x_ref[...] + jnp.dot(y, wo_ref[...], preferred_element_type=F32)
    h = _rms(x, g_ref[...]).astype(BF16)
    gate = jnp.dot(h, wg_ref[...], preferred_element_type=F32)
    up = jnp.dot(h, wu_ref[...], preferred_element_type=F32)
    act = (jax.nn.silu(gate) * up).astype(BF16)
    x = x + jnp.dot(act, wd_ref[...], preferred_element_type=F32)
    if final_norm:
        x = _rms(x, gf_ref[...])
    out_ref[...] = x


def _out_ffn(x, ys, w_out, g, wg, wu, wd, g_final=None):
    B, S, D = x.shape
    rows = B * S
    tm = ROW_TILE
    x2 = x.reshape(rows, D)
    ys2 = [y.reshape(rows, y.shape[-1]) for y in ys]
    consts = [w_out, g, wg, wu, wd] + ([g_final] if g_final is not None else [])
    out = pl.pallas_call(
        functools.partial(_out_ffn_kernel, len(ys), g_final is not None),
        grid=(rows // tm,),
        in_specs=([pl.BlockSpec((tm, D), lambda i: (i, 0))]
                  + [pl.BlockSpec((tm, y.shape[-1]), lambda i: (i, 0)) for y in ys2]
                  + [pl.BlockSpec(t.shape, lambda i, nd=t.ndim: (0,) * nd,
                                  pipeline_mode=pl.Buffered(1)) for t in consts]),
        out_specs=pl.BlockSpec((tm, D), lambda i: (i, 0)),
        out_shape=jax.ShapeDtypeStruct((rows, D), F32),
        compiler_params=_params("parallel"),
        name="out_proj_ffn",
    )(x2, *ys2, *consts)
    return out.reshape(B, S, D)


def _mla_in_kernel(x_ref, g_ref, win_ref, qn_ref, kvn_ref, wq_ref, wk_ref, wvt_ref, cos_ref,
                   sin_ref, q_ref, k_ref, vt_ref):
    h = _rms(x_ref[...], g_ref[...]).astype(BF16)
    proj = jnp.dot(h, win_ref[...], preferred_element_type=F32)
    cq = _rms(proj[:, :MLA_Q_RANK], qn_ref[...]).astype(BF16)
    ckv = _rms(proj[:, MLA_Q_RANK:MLA_Q_RANK + MLA_KV_RANK], kvn_ref[...]).astype(BF16)
    base = MLA_Q_RANK + MLA_KV_RANK
    cos = cos_ref[...]
    sin = sin_ref[...]
    k_rope = proj[:, base:base + LANES] * cos + proj[:, base + LANES:base + 2 * LANES] * sin
    scale = math.log2(math.e) / math.sqrt(MLA_QK)
    width = MLA_HEADS * LANES
    for hd in range(MLA_HEADS):
        cols = slice(hd * LANES, (hd + 1) * LANES)
        rot_cols = slice(width + hd * LANES, width + (hd + 1) * LANES)
        qa = jnp.dot(cq, wq_ref[:, cols], preferred_element_type=F32)
        qb = jnp.dot(cq, wq_ref[:, rot_cols], preferred_element_type=F32)
        q_ref[:, cols] = ((qa * cos + qb * sin) * scale).astype(BF16)
        kn = jnp.dot(ckv, wk_ref[:, cols], preferred_element_type=F32)
        k_ref[:, cols] = (kn + k_rope).astype(BF16)
    vt_ref[...] = lax.dot_general(wvt_ref[...], ckv, (((1,), (1,)), ((), ())),
                                  preferred_element_type=F32).astype(BF16)


def _mla_in_proj(x, g, win, qn, kvn, wq, wk, wvt, cos_tab, sin_tab):
    B, S, D = x.shape
    tm = ROW_TILE
    width = MLA_HEADS * LANES
    consts = [g, win, qn, kvn, wq, wk, wvt]
    return pl.pallas_call(
        _mla_in_kernel,
        grid=(B, S // tm),
        in_specs=([pl.BlockSpec((None, tm, D), lambda b, i: (b, i, 0))]
                  + [_const_spec(t.shape) for t in consts]
                  + [pl.BlockSpec((tm, LANES), lambda b, i: (i, 0)),
                     pl.BlockSpec((tm, LANES), lambda b, i: (i, 0))]),
        out_specs=[
            pl.BlockSpec((None, tm, width), lambda b, i: (b, i, 0)),
            pl.BlockSpec((None, tm, width), lambda b, i: (b, i, 0)),
            pl.BlockSpec((None, None, MLA_HEADS * MLA_V, tm), lambda b, i: (b, i, 0, 0)),
        ],
        out_shape=[
            jax.ShapeDtypeStruct((B, S, width), BF16),
            jax.ShapeDtypeStruct((B, S, width), BF16),
            jax.ShapeDtypeStruct((B, S // tm, MLA_HEADS * MLA_V, tm), BF16),
        ],
        compiler_params=_params("parallel", "parallel"),
        name="mla_in_proj",
    )(x, *consts, cos_tab, sin_tab)


def _mla_attn_kernel(q_ref, k_ref, vt_ref, o_ref, st_scr, acc_scr):
    n_kb, _, bk = vt_ref.shape
    bq = q_ref.shape[0]
    qs = [q_ref[:, hh * LANES:(hh + 1) * LANES] for hh in range(2)]

    def scores(j, slot):
        k0 = pl.multiple_of(j * bk, bk)
        for hh in range(2):
            kb = k_ref[pl.ds(k0, bk), hh * LANES:(hh + 1) * LANES]
            st_scr[slot, hh] = lax.dot_general(kb, qs[hh], (((1,), (1,)), ((), ())),
                                               preferred_element_type=F32)

    def update(j, slot, carry):
        new = []
        for hh in range(2):
            m, l = carry[hh]
            st = st_scr[slot, hh]
            m_new = jnp.maximum(m, jnp.max(st, axis=0, keepdims=True))
            alpha = jnp.exp2(m - m_new)
            p = jnp.exp2(st - m_new)
            l = alpha * l + jnp.sum(p, axis=0, keepdims=True)
            vb = vt_ref[j, hh * MLA_V:(hh + 1) * MLA_V, :]
            acc_scr[hh] = alpha * acc_scr[hh] + jnp.dot(vb, p.astype(BF16),
                                                        preferred_element_type=F32)
            new.append((m_new, l))
        return tuple(new)

    acc_scr[...] = jnp.zeros_like(acc_scr)
    scores(0, 0)

    def body(jj, carry):
        j = 2 * jj
        scores(j + 1, 1)
        carry = update(j, 0, carry)
        scores(j + 2, 0)
        return update(j + 1, 1, carry)

    init = tuple((jnp.full((1, bq), NEG_BIG, F32), jnp.zeros((1, bq), F32)) for _ in range(2))
    carry = lax.fori_loop(0, n_kb // 2 - 1, body, init)
    scores(n_kb - 1, 1)
    carry = update(n_kb - 2, 0, carry)
    carry = update(n_kb - 1, 1, carry)
    outs = [acc_scr[hh] / carry[hh][1] for hh in range(2)]
    o_ref[...] = jnp.concatenate(outs, axis=0).T.astype(o_ref.dtype)


def _mla_attention(q, k, vt):
    B, S, _ = q.shape
    n_kb = vt.shape[1]
    bq = MLA_Q_TILE
    n_pairs = MLA_HEADS // 2
    return pl.pallas_call(
        _mla_attn_kernel,
        grid=(B, n_pairs, S // bq),
        in_specs=[
            pl.BlockSpec((None, bq, 2 * LANES), lambda b, p, i: (b, i, p)),
            pl.BlockSpec((None, S, 2 * LANES), lambda b, p, i: (b, 0, p)),
            pl.BlockSpec((None, n_kb, 2 * MLA_V, MLA_K_TILE), lambda b, p, i: (b, 0, p, 0)),
        ],
        out_specs=pl.BlockSpec((None, bq, 2 * MLA_V), lambda b, p, i: (b, i, p)),
        out_shape=jax.ShapeDtypeStruct((B, S, MLA_HEADS * MLA_V), BF16),
        scratch_shapes=[
            pltpu.VMEM((2, 2, MLA_K_TILE, bq), F32),
            pltpu.VMEM((2, MLA_V, bq), F32),
        ],
        compiler_params=_params("parallel", "parallel", "arbitrary"),
        name="mla_attention",
    )(q, k, vt)


def _block_diag(w):
    nb, d, _ = w.shape
    eye = jnp.eye(nb, dtype=w.dtype)
    return (eye[:, None, :, None] * w[:, :, None, :]).reshape(nb * d, nb * d)


def _rot_cols(w):
    half = MLA_ROPE // 2
    return jnp.concatenate([-w[..., half:], w[..., :half]], axis=-1)


def _pad_lanes(w, offset):
    return jnp.pad(w, [(0, 0)] * (w.ndim - 1) + [(offset, LANES - offset - w.shape[-1])])


def _prepare(norm_mix, norm_ffn, norm_final, ab_w_in, ab_conv_w, ab_conv_b, rg_w_a, rg_b_a,
             rg_w_i, rg_b_i, rg_lam, ab_w_out, mla_w_in, mla_q_norm, mla_w_qb, mla_kv_norm,
             mla_w_kvb, mla_w_out, ffn_w_gate, ffn_w_up, ffn_w_down):
    row = lambda v: v.reshape(1, -1).astype(F32)
    P = {}
    P["norm_mix"] = [row(norm_mix[l]) for l in range(2)]
    P["norm_ffn"] = [row(norm_ffn[l]) for l in range(2)]
    P["norm_final"] = row(norm_final)
    P["ab_w_in"] = ab_w_in[0].astype(BF16)
    P["conv_w"] = ab_conv_w[0].astype(F32)
    P["conv_b"] = row(ab_conv_b[0])
    P["rg_w"] = [jnp.concatenate([_block_diag(rg_w_a[0, d]), _block_diag(rg_w_i[0, d])],
                                 axis=1).astype(BF16) for d in range(2)]
    P["rg_b"] = [jnp.concatenate([rg_b_a[0, d], rg_b_i[0, d]]).reshape(1, -1) for d in range(2)]
    P["rg_lam"] = [row(rg_lam[0, d]) for d in range(2)]
    P["ab_w_out"] = ab_w_out[0].astype(BF16)

    w_in = mla_w_in[0]
    base = MLA_Q_RANK + MLA_KV_RANK
    w_kr = w_in[:, base:]
    P["mla_w_in"] = jnp.concatenate(
        [w_in[:, :base], _pad_lanes(w_kr, MLA_NOPE), _pad_lanes(_rot_cols(w_kr), MLA_NOPE)],
        axis=1).astype(BF16)
    P["mla_q_norm"] = row(mla_q_norm[0])
    P["mla_kv_norm"] = row(mla_kv_norm[0])
    wq = mla_w_qb[0].reshape(MLA_Q_RANK, MLA_HEADS, MLA_QK)
    wq_plain = _pad_lanes(wq, 0)
    wq_rot = _pad_lanes(_rot_cols(wq[..., MLA_NOPE:]), MLA_NOPE)
    P["mla_wq"] = jnp.concatenate(
        [wq_plain.reshape(MLA_Q_RANK, -1), wq_rot.reshape(MLA_Q_RANK, -1)], axis=1).astype(BF16)
    wkv = mla_w_kvb[0].reshape(MLA_KV_RANK, MLA_HEADS, MLA_NOPE + MLA_V)
    P["mla_wk"] = _pad_lanes(wkv[..., :MLA_NOPE], 0).reshape(MLA_KV_RANK, -1).astype(BF16)
    P["mla_wvt"] = wkv[..., MLA_NOPE:].reshape(MLA_KV_RANK, -1).T.astype(BF16)
    P["mla_w_out"] = mla_w_out[0].astype(BF16)
    P["ffn"] = [(ffn_w_gate[l].astype(BF16), ffn_w_up[l].astype(BF16),
                 ffn_w_down[l].astype(BF16)) for l in range(2)]
    return P


def _rope_slabs(S):
    inv_freq = 1.0 / (ROPE_THETA ** (jnp.arange(0, MLA_ROPE, 2, dtype=F32) / MLA_ROPE))
    ang = jnp.arange(S, dtype=F32)[:, None] * inv_freq[None, :]
    cos, sin = jnp.cos(ang), jnp.sin(ang)
    pad = LANES - MLA_QK
    cos_tab = jnp.concatenate([jnp.ones((S, MLA_NOPE), F32), cos, cos, jnp.zeros((S, pad), F32)], 1)
    sin_tab = jnp.concatenate([jnp.zeros((S, MLA_NOPE), F32), sin, sin, jnp.zeros((S, pad), F32)], 1)
    return cos_tab, sin_tab


def _trunk(x, P):
    S = x.shape[1]
    xg, qkv = _ab_in_proj(x, P["norm_mix"][0], P["ab_w_in"])
    hf = _rglru_pass(xg, None, P["conv_w"], P["conv_b"], P["rg_w"][0], P["rg_b"][0],
                     P["rg_lam"][0], reverse=False)
    y_rnn = _rglru_pass(xg, hf, P["conv_w"], P["conv_b"], P["rg_w"][1], P["rg_b"][1],
                        P["rg_lam"][1], reverse=True)
    o = _dilated_attention(qkv)
    x = _out_ffn(x, [y_rnn, o], P["ab_w_out"], P["norm_ffn"][0], *P["ffn"][0])
    cos_tab, sin_tab = _rope_slabs(S)
    q, k, vt = _mla_in_proj(x, P["norm_mix"][1], P["mla_w_in"], P["mla_q_norm"], P["mla_kv_norm"],
                            P["mla_wq"], P["mla_wk"], P["mla_wvt"], cos_tab, sin_tab)
    o = _mla_attention(q, k, vt)
    return _out_ffn(x, [o], P["mla_w_out"], P["norm_ffn"][1], *P["ffn"][1],
                    g_final=P["norm_final"])


def kernel(x_prompt, x_sample, norm_mix, norm_ffn, norm_final, ab_w_in, ab_conv_w, ab_conv_b, rg_w_a, rg_b_a, rg_w_i, rg_b_i, rg_lam, ab_w_out, mla_w_in, mla_q_norm, mla_w_qb, mla_kv_norm, mla_w_kvb, mla_w_out, ffn_w_gate, ffn_w_up, ffn_w_down):
    P = _prepare(norm_mix, norm_ffn, norm_final, ab_w_in, ab_conv_w, ab_conv_b, rg_w_a, rg_b_a,
                 rg_w_i, rg_b_i, rg_lam, ab_w_out, mla_w_in, mla_q_norm, mla_w_qb, mla_kv_norm,
                 mla_w_kvb, mla_w_out, ffn_w_gate, ffn_w_up, ffn_w_down)
    return (_trunk(x_prompt, P), _trunk(x_sample, P))
```

```python
import functools
import math

import jax
import jax.numpy as jnp
from jax import lax
from jax.experimental import pallas as pl
from jax.experimental.pallas import tpu as pltpu

F32 = jnp.float32
BF16 = jnp.bfloat16

D_MODEL = 1024
EPS = 1e-6
RG_WIDTH = 512
RG_BLOCKS = 8
RG_C = 8.0
DA_HEADS = 8
DA_HEAD_DIM = 64
DA_WIDTH = 512
DA_DILATIONS = (1, 4, 16)
DA_HALF_STEPS = 64
DA_TOKENS = DA_HALF_STEPS * max(DA_DILATIONS)
MLA_HEADS = 16
MLA_Q_RANK = 384
MLA_KV_RANK = 256
MLA_NOPE = 64
MLA_ROPE = 32
MLA_V = 64
MLA_QK = MLA_NOPE + MLA_ROPE
ROPE_THETA = 10000.0
FFN_HIDDEN = 2816
NEG_BIG = -1e30
LANES = 128

VMEM_LIMIT = 56 * 1024 * 1024

ROW_TILE = 512
SCAN_TILE = 512
MLA_Q_TILE = 512
MLA_K_TILE = ROW_TILE


def _params(*sem):
    return pltpu.CompilerParams(dimension_semantics=sem, vmem_limit_bytes=VMEM_LIMIT)


def _const_spec(shape):
    nd = len(shape)
    return pl.BlockSpec(shape, lambda *_: (0,) * nd)


def _rms(x, g):
    return x * lax.rsqrt(jnp.mean(x * x, axis=-1, keepdims=True) + EPS) * g


def _ab_in_kernel(x_ref, g_ref, w_ref, xg_ref, qkv_ref):
    h = _rms(x_ref[...], g_ref[...]).astype(BF16)
    xg_ref[...] = jnp.dot(h, w_ref[:, :2 * RG_WIDTH], preferred_element_type=F32)
    n_slabs = 3 * DA_WIDTH // LANES
    qkv = jnp.dot(h, w_ref[:, 2 * RG_WIDTH:], preferred_element_type=F32)
    for j in range(n_slabs):
        qkv_ref[j] = qkv[:, LANES * j:LANES * (j + 1)]


def _ab_in_proj(x, g, w):
    B, S, D = x.shape
    n_slabs = 3 * DA_WIDTH // LANES
    tm = ROW_TILE
    return pl.pallas_call(
        _ab_in_kernel,
        grid=(B, S // tm),
        in_specs=[
            pl.BlockSpec((None, tm, D), lambda b, i: (b, i, 0)),
            _const_spec(g.shape),
            _const_spec(w.shape),
        ],
        out_specs=[
            pl.BlockSpec((None, tm, 2 * RG_WIDTH), lambda b, i: (b, i, 0)),
            pl.BlockSpec((None, n_slabs, tm, LANES), lambda b, i: (b, 0, i, 0)),
        ],
        out_shape=[
            jax.ShapeDtypeStruct((B, S, 2 * RG_WIDTH), F32),
            jax.ShapeDtypeStruct((B, n_slabs, S, LANES), F32),
        ],
        compiler_params=_params("parallel", "parallel"),
        name="ab_in_proj",
    )(x, g, w)


def _rglru_kernel(reverse, n_chunks, *refs):
    if reverse:
        (x_ref, xp_ref, xn_ref, gate_ref, hf_ref, cw_ref, cb_ref, wg_ref, bg_ref, lam_ref,
         out_ref, carry_ref) = refs
    else:
        (x_ref, xp_ref, xn_ref, cw_ref, cb_ref, wg_ref, bg_ref, lam_ref,
         out_ref, carry_ref) = refs
    i = pl.program_id(1)
    c = (n_chunks - 1 - i) if reverse else i
    tc, width = x_ref.shape

    @pl.when(i == 0)
    def _():
        carry_ref[...] = jnp.zeros_like(carry_ref)

    x = x_ref[...]
    row = lax.broadcasted_iota(jnp.int32, (tc, width), 0)
    prev = jnp.where(c > 0, xp_ref[...], 0.0)
    nxt = jnp.where(c < n_chunks - 1, xn_ref[...], 0.0)
    p7, p6, n0 = prev[7:8], prev[6:7], nxt[0:1]
    xm1 = jnp.where(row == 0, p7, pltpu.roll(x, 1, 0))
    xm2 = jnp.where(row == 0, p6, jnp.where(row == 1, p7, pltpu.roll(x, 2, 0)))
    xp1 = jnp.where(row == tc - 1, n0, pltpu.roll(x, tc - 1, 0))
    cw = cw_ref[...]
    xc = cw[0:1] * xm2 + cw[1:2] * xm1 + cw[2:3] * x + cw[3:4] * xp1 + cb_ref[...]

    z = jnp.dot(xc.astype(BF16), wg_ref[...], preferred_element_type=F32) + bg_ref[...]
    r = jax.nn.sigmoid(z[:, :width])
    gi = jax.nn.sigmoid(z[:, width:])
    log_a = (-RG_C) * r * jax.nn.softplus(-lam_ref[...])
    a = jnp.exp(log_a)
    u = jnp.sqrt(-jnp.tanh(log_a) * (a * a + 1.0)) * (gi * xc)

    d = 1
    while d < tc:
        shift = (tc - d) if reverse else d
        valid = (row < tc - d) if reverse else (row >= d)
        a_sh = pltpu.roll(a, shift, 0)
        u_sh = pltpu.roll(u, shift, 0)
        u = jnp.where(valid, a * u_sh + u, u)
        a = jnp.where(valid, a * a_sh, a)
        d *= 2
    h = u + a * carry_ref[...]
    carry_ref[...] = h[0:1] if reverse else h[tc - 1:tc]

    if reverse:
        out_ref[...] = (jax.nn.gelu(gate_ref[...]) * (hf_ref[...] + h)).astype(out_ref.dtype)
    else:
        out_ref[...] = h


def _rglru_pass(xg, hf, conv_w, conv_b, wg, bg, lam, reverse):
    B, S, _ = xg.shape
    tc = SCAN_TILE
    n_chunks = S // tc
    sub = 8

    def chunk(i):
        return (n_chunks - 1 - i) if reverse else i

    x_spec = pl.BlockSpec((None, tc, RG_WIDTH), lambda b, i: (b, chunk(i), 0))
    prev_spec = pl.BlockSpec(
        (None, sub, RG_WIDTH), lambda b, i: (b, jnp.maximum(chunk(i) * (tc // sub) - 1, 0), 0))
    next_spec = pl.BlockSpec(
        (None, sub, RG_WIDTH),
        lambda b, i: (b, jnp.minimum((chunk(i) + 1) * (tc // sub), S // sub - 1), 0))
    consts = [conv_w, conv_b, wg, bg, lam]
    const_specs = [_const_spec(t.shape) for t in consts]
    if reverse:
        gate_spec = pl.BlockSpec((None, tc, RG_WIDTH), lambda b, i: (b, chunk(i), 1))
        hf_spec = pl.BlockSpec((None, tc, RG_WIDTH), lambda b, i: (b, chunk(i), 0))
        inputs = [xg, xg, xg, xg, hf] + consts
        in_specs = [x_spec, prev_spec, next_spec, gate_spec, hf_spec] + const_specs
        out_dtype = BF16
    else:
        inputs = [xg, xg, xg] + consts
        in_specs = [x_spec, prev_spec, next_spec] + const_specs
        out_dtype = F32
    return pl.pallas_call(
        functools.partial(_rglru_kernel, reverse, n_chunks),
        grid=(B, n_chunks),
        in_specs=in_specs,
        out_specs=pl.BlockSpec((None, tc, RG_WIDTH), lambda b, i: (b, chunk(i), 0)),
        out_shape=jax.ShapeDtypeStruct((B, S, RG_WIDTH), out_dtype),
        scratch_shapes=[pltpu.VMEM((1, RG_WIDTH), F32)],
        compiler_params=_params("parallel", "arbitrary"),
        name="rglru_bwd" if reverse else "rglru_fwd",
    )(*inputs)


def _dilated_kernel(n_blocks, slopes, q_ref, kp_ref, kc_ref, kn_ref, vp_ref, vc_ref, vn_ref,
                    out_ref, qs_ref, ks_ref, vs_ref, o_scr, lse_scr, bias_scr):
    i = pl.program_id(1)
    T = DA_TOKENS
    W = DA_HALF_STEPS
    n_pairs = DA_WIDTH // LANES
    log2e = math.log2(math.e)
    scale = log2e / math.sqrt(DA_HEAD_DIM)
    is_first = i == 0
    is_last = i == n_blocks - 1

    for g, dil in enumerate(DA_DILATIONS):
        n = T // dil
        bq = min(n, 128)
        bk = bq + 2 * W
        rowq = lax.broadcasted_iota(jnp.int32, (bq, bk), 0)
        colk = lax.broadcasted_iota(jnp.int32, (bq, bk), 1)
        rel = colk - W - rowq
        band = jnp.abs(rel) <= W
        neg_dist = -(jnp.abs(rel) * dil).astype(F32)
        lane = lax.broadcasted_iota(jnp.int32, (bq, LANES), 1)
        lo_lane = lane < DA_HEAD_DIM
        for head in range(DA_HEADS):
            bias_scr[head, 0:bq, 0:bk] = jnp.where(band, (slopes[head] * log2e) * neg_dist,
                                                   NEG_BIG)

        def class_body(r, carry, dil=dil, n=n, bq=bq, bk=bk, g=g, colk=colk,
                       lo_lane=lo_lane):
            for p in range(n_pairs):
                qs_ref[p, 0:n] = (q_ref[p, pl.ds(r, n, stride=dil), :] * scale).astype(BF16)
                for src_p, src_c, src_n, dst in ((kp_ref, kc_ref, kn_ref, ks_ref),
                                                 (vp_ref, vc_ref, vn_ref, vs_ref)):
                    dst[p, 0:W] = src_p[p, pl.ds(r + dil * (n - W), W, stride=dil), :].astype(BF16)
                    dst[p, W:W + n] = src_c[p, pl.ds(r, n, stride=dil), :].astype(BF16)
                    dst[p, W + n:2 * W + n] = src_n[p, pl.ds(r, W, stride=dil), :].astype(BF16)

            def qb_body(qb, carry2):
                q0 = pl.multiple_of(qb * bq, bq)
                pos = q0 + colk
                in_seq = jnp.logical_not(
                    jnp.logical_or(jnp.logical_and(is_first, pos < W),
                                   jnp.logical_and(is_last, pos >= n + W)))
                scores = []
                for p in range(n_pairs):
                    qp = qs_ref[p, pl.ds(q0, bq), :]
                    kp = ks_ref[p, pl.ds(q0, bk), :]
                    for hh in range(2):
                        qh = jnp.where(lo_lane if hh == 0 else jnp.logical_not(lo_lane), qp,
                                       jnp.zeros_like(qp))
                        scores.append(lax.dot_general(qh, kp, (((1,), (1,)), ((), ())),
                                                      preferred_element_type=F32))
                probs, lse2, inv_l = [], [], []
                for head, s in enumerate(scores):
                    s = jnp.where(in_seq, s + bias_scr[head, 0:bq, 0:bk], NEG_BIG)
                    m = jnp.max(s, axis=-1, keepdims=True)
                    e = jnp.exp2(s - m)
                    l = jnp.sum(e, axis=-1, keepdims=True)
                    probs.append(e.astype(BF16))
                    inv_l.append(1.0 / l)
                    lse2.append(m + jnp.log2(l))
                for p in range(n_pairs):
                    vp = vs_ref[p, pl.ds(q0, bk), :]
                    o_h = [jnp.dot(probs[2 * p + hh], vp, preferred_element_type=F32)
                           * inv_l[2 * p + hh] for hh in range(2)]
                    rows = pl.ds(r + dil * q0, bq, stride=dil)
                    o_scr[g, p, rows, :] = jnp.where(lo_lane, o_h[0], o_h[1])
                    lse_scr[g, p, rows, :] = jnp.where(lo_lane, lse2[2 * p], lse2[2 * p + 1])
                return carry2

            lax.fori_loop(0, n // bq, qb_body, 0)
            return carry

        lax.fori_loop(0, dil, class_body, 0)

    rows_per_step = 256

    def merge_body(t, carry):
        t0 = pl.multiple_of(t * rows_per_step, rows_per_step)
        rows = pl.ds(t0, rows_per_step)
        for p in range(n_pairs):
            lse = [lse_scr[g, p, rows, :] for g in range(len(DA_DILATIONS))]
            mx = jnp.maximum(jnp.maximum(lse[0], lse[1]), lse[2])
            w = [jnp.exp2(v - mx) for v in lse]
            num = sum(w[g] * o_scr[g, p, rows, :] for g in range(len(DA_DILATIONS)))
            out_ref[rows, LANES * p:LANES * (p + 1)] = (num / (w[0] + w[1] + w[2])).astype(
                out_ref.dtype)
        return carry

    lax.fori_loop(0, T // rows_per_step, merge_body, 0)


def _dilated_attention(qkv):
    B, _, S, _ = qkv.shape
    T = DA_TOKENS
    n_blocks = S // T
    n_pairs = DA_WIDTH // LANES
    slopes = tuple(2.0 ** (-8.0 * (h + 1) / DA_HEADS) for h in range(DA_HEADS))

    def spec(slab, shift):
        def index(b, i):
            return (b, slab, jnp.clip(i + shift, 0, n_blocks - 1), 0)
        return pl.BlockSpec((None, n_pairs, T, LANES), index)

    win = T + 2 * DA_HALF_STEPS
    return pl.pallas_call(
        functools.partial(_dilated_kernel, n_blocks, slopes),
        grid=(B, n_blocks),
        in_specs=[spec(0, 0), spec(1, -1), spec(1, 0), spec(1, 1),
                  spec(2, -1), spec(2, 0), spec(2, 1)],
        out_specs=pl.BlockSpec((None, T, DA_WIDTH), lambda b, i: (b, i, 0)),
        out_shape=jax.ShapeDtypeStruct((B, S, DA_WIDTH), BF16),
        scratch_shapes=[
            pltpu.VMEM((n_pairs, T, LANES), BF16),
            pltpu.VMEM((n_pairs, win, LANES), BF16),
            pltpu.VMEM((n_pairs, win, LANES), BF16),
            pltpu.VMEM((len(DA_DILATIONS), n_pairs, T, LANES), F32),
            pltpu.VMEM((len(DA_DILATIONS), n_pairs, T, LANES), F32),
            pltpu.VMEM((DA_HEADS, 128, 128 + 2 * DA_HALF_STEPS), F32),
        ],
        compiler_params=_params("parallel", "parallel"),
        name="dilated_attention",
    )(qkv, qkv, qkv, qkv, qkv, qkv, qkv)


def _out_ffn_kernel(n_y, final_norm, *refs):
    x_ref = refs[0]
    y_refs = refs[1:1 + n_y]
    wo_ref, g_ref, wg_ref, wu_ref, wd_ref = refs[1 + n_y:6 + n_y]
    gf_ref = refs[6 + n_y] if final_norm else None
    out_ref = refs[-1]

    y = y_refs[0][...] if n_y == 1 else jnp.concatenate([r[...] for r in y_refs], axis=1)
    x = x_ref[...] + jnp.dot(y, wo_ref[...], preferred_element_type=F32)
    h = _rms(x, g_ref[...]).astype(BF16)
    gate = jnp.dot(h, wg_ref[...], preferred_element_type=F32)
    up = jnp.dot(h, wu_ref[...], preferred_element_type=F32)
    act = (jax.nn.silu(gate) * up).astype(BF16)
    x = x + jnp.dot(act, wd_ref[...], preferred_element_type=F32)
    if final_norm:
        x = _rms(x, gf_ref[...])
    out_ref[...] = x


def _out_ffn(x, ys, w_out, g, wg, wu, wd, g_final=None):
    B, S, D = x.shape
    rows = B * S
    tm = ROW_TILE
    x2 = x.reshape(rows, D)
    ys2 = [y.reshape(rows, y.shape[-1]) for y in ys]
    consts = [w_out, g, wg, wu, wd] + ([g_final] if g_final is not None else [])
    out = pl.pallas_call(
        functools.partial(_out_ffn_kernel, len(ys), g_final is not None),
        grid=(rows // tm,),
        in_specs=([pl.BlockSpec((tm, D), lambda i: (i, 0))]
                  + [pl.BlockSpec((tm, y.shape[-1]), lambda i: (i, 0)) for y in ys2]
                  + [pl.BlockSpec(t.shape, lambda i, nd=t.ndim: (0,) * nd,
                                  pipeline_mode=pl.Buffered(1)) for t in consts]),
        out_specs=pl.BlockSpec((tm, D), lambda i: (i, 0)),
        out_shape=jax.ShapeDtypeStruct((rows, D), F32),
        compiler_params=_params("parallel"),
        name="out_proj_ffn",
    )(x2, *ys2, *consts)
    return out.reshape(B, S, D)


def _mla_in_kernel(x_ref, g_ref, win_ref, qn_ref, kvn_ref, wq_ref, wk_ref, wvt_ref, cos_ref,
                   sin_ref, q_ref, k_ref, vt_ref):
    h = _rms(x_ref[...], g_ref[...]).astype(BF16)
    proj = jnp.dot(h, win_ref[...], preferred_element_type=F32)
    cq = _rms(proj[:, :MLA_Q_RANK], qn_ref[...]).astype(BF16)
    ckv = _rms(proj[:, MLA_Q_RANK:MLA_Q_RANK + MLA_KV_RANK], kvn_ref[...]).astype(BF16)
    base = MLA_Q_RANK + MLA_KV_RANK
    cos = cos_ref[...]
    sin = sin_ref[...]
    k_rope = proj[:, base:base + LANES] * cos + proj[:, base + LANES:base + 2 * LANES] * sin
    scale = math.log2(math.e) / math.sqrt(MLA_QK)
    width = MLA_HEADS * LANES
    for hd in range(MLA_HEADS):
        cols = slice(hd * LANES, (hd + 1) * LANES)
        rot_cols = slice(width + hd * LANES, width + (hd + 1) * LANES)
        qa = jnp.dot(cq, wq_ref[:, cols], preferred_element_type=F32)
        qb = jnp.dot(cq, wq_ref[:, rot_cols], preferred_element_type=F32)
        q_ref[:, cols] = ((qa * cos + qb * sin) * scale).astype(BF16)
        kn = jnp.dot(ckv, wk_ref[:, cols], preferred_element_type=F32)
        k_ref[:, cols] = (kn + k_rope).astype(BF16)
    vt_ref[...] = lax.dot_general(wvt_ref[...], ckv, (((1,), (1,)), ((), ())),
                                  preferred_element_type=F32).astype(BF16)


def _mla_in_proj(x, g, win, qn, kvn, wq, wk, wvt, cos_tab, sin_tab):
    B, S, D = x.shape
    tm = ROW_TILE
    width = MLA_HEADS * LANES
    consts = [g, win, qn, kvn, wq, wk, wvt]
    return pl.pallas_call(
        _mla_in_kernel,
        grid=(B, S // tm),
        in_specs=([pl.BlockSpec((None, tm, D), lambda b, i: (b, i, 0))]
                  + [_const_spec(t.shape) for t in consts]
                  + [pl.BlockSpec((tm, LANES), lambda b, i: (i, 0)),
                     pl.BlockSpec((tm, LANES), lambda b, i: (i, 0))]),
        out_specs=[
            pl.BlockSpec((None, tm, width), lambda b, i: (b, i, 0)),
            pl.BlockSpec((None, tm, width), lambda b, i: (b, i, 0)),
            pl.BlockSpec((None, None, MLA_HEADS * MLA_V, tm), lambda b, i: (b, i, 0, 0)),
        ],
        out_shape=[
            jax.ShapeDtypeStruct((B, S, width), BF16),
            jax.ShapeDtypeStruct((B, S, width), BF16),
            jax.ShapeDtypeStruct((B, S // tm, MLA_HEADS * MLA_V, tm), BF16),
        ],
        compiler_params=_params("parallel", "parallel"),
        name="mla_in_proj",
    )(x, *consts, cos_tab, sin_tab)


def _mla_attn_kernel(q_ref, k_ref, vt_ref, o_ref, st_scr, acc_scr):
    n_kb, _, bk = vt_ref.shape
    bq = q_ref.shape[0]
    qs = [q_ref[:, hh * LANES:(hh + 1) * LANES] for hh in range(2)]

    def scores(j, slot):
        k0 = pl.multiple_of(j * bk, bk)
        for hh in range(2):
            kb = k_ref[pl.ds(k0, bk), hh * LANES:(hh + 1) * LANES]
            st_scr[slot, hh] = lax.dot_general(kb, qs[hh], (((1,), (1,)), ((), ())),
                                               preferred_element_type=F32)

    def update(j, slot, carry):
        new = []
        for hh in range(2):
            m, l = carry[hh]
            st = st_scr[slot, hh]
            m_new = jnp.maximum(m, jnp.max(st, axis=0, keepdims=True))
            alpha = jnp.exp2(m - m_new)
            p = jnp.exp2(st - m_new)
            l = alpha * l + jnp.sum(p, axis=0, keepdims=True)
            vb = vt_ref[j, hh * MLA_V:(hh + 1) * MLA_V, :]
            acc_scr[hh] = alpha * acc_scr[hh] + jnp.dot(vb, p.astype(BF16),
                                                        preferred_element_type=F32)
            new.append((m_new, l))
        return tuple(new)

    acc_scr[...] = jnp.zeros_like(acc_scr)
    scores(0, 0)

    def body(jj, carry):
        j = 2 * jj
        scores(j + 1, 1)
        carry = update(j, 0, carry)
        scores(j + 2, 0)
        return update(j + 1, 1, carry)

    init = tuple((jnp.full((1, bq), NEG_BIG, F32), jnp.zeros((1, bq), F32)) for _ in range(2))
    carry = lax.fori_loop(0, n_kb // 2 - 1, body, init)
    scores(n_kb - 1, 1)
    carry = update(n_kb - 2, 0, carry)
    carry = update(n_kb - 1, 1, carry)
    outs = [acc_scr[hh] / carry[hh][1] for hh in range(2)]
    o_ref[...] = jnp.concatenate(outs, axis=0).T.astype(o_ref.dtype)


def _mla_attention(q, k, vt):
    B, S, _ = q.shape
    n_kb = vt.shape[1]
    bq = MLA_Q_TILE
    n_pairs = MLA_HEADS // 2
    return pl.pallas_call(
        _mla_attn_kernel,
        grid=(B, n_pairs, S // bq),
        in_specs=[
            pl.BlockSpec((None, bq, 2 * LANES), lambda b, p, i: (b, i, p)),
            pl.BlockSpec((None, S, 2 * LANES), lambda b, p, i: (b, 0, p)),
            pl.BlockSpec((None, n_kb, 2 * MLA_V, MLA_K_TILE), lambda b, p, i: (b, 0, p, 0)),
        ],
        out_specs=pl.BlockSpec((None, bq, 2 * MLA_V), lambda b, p, i: (b, i, p)),
        out_shape=jax.ShapeDtypeStruct((B, S, MLA_HEADS * MLA_V), BF16),
        scratch_shapes=[
            pltpu.VMEM((2, 2, MLA_K_TILE, bq), F32),
            pltpu.VMEM((2, MLA_V, bq), F32),
        ],
        compiler_params=_params("parallel", "parallel", "arbitrary"),
        name="mla_attention",
    )(q, k, vt)


def _block_diag(w):
    nb, d, _ = w.shape
    eye = jnp.eye(nb, dtype=w.dtype)
    return (eye[:, None, :, None] * w[:, :, None, :]).reshape(nb * d, nb * d)


def _rot_cols(w):
    half = MLA_ROPE // 2
    return jnp.concatenate([-w[..., half:], w[..., :half]], axis=-1)


def _pad_lanes(w, offset):
    return jnp.pad(w, [(0, 0)] * (w.ndim - 1) + [(offset, LANES - offset - w.shape[-1])])


def _prepare(norm_mix, norm_ffn, norm_final, ab_w_in, ab_conv_w, ab_conv_b, rg_w_a, rg_b_a,
             rg_w_i, rg_b_i, rg_lam, ab_w_out, mla_w_in, mla_q_norm, mla_w_qb, mla_kv_norm,
             mla_w_kvb, mla_w_out, ffn_w_gate, ffn_w_up, ffn_w_down):
    row = lambda v: v.reshape(1, -1).astype(F32)
    P = {}
    P["norm_mix"] = [row(norm_mix[l]) for l in range(2)]
    P["norm_ffn"] = [row(norm_ffn[l]) for l in range(2)]
    P["norm_final"] = row(norm_final)
    P["ab_w_in"] = ab_w_in[0].astype(BF16)
    P["conv_w"] = ab_conv_w[0].astype(F32)
    P["conv_b"] = row(ab_conv_b[0])
    P["rg_w"] = [jnp.concatenate([_block_diag(rg_w_a[0, d]), _block_diag(rg_w_i[0, d])],
                                 axis=1).astype(BF16) for d in range(2)]
    P["rg_b"] = [jnp.concatenate([rg_b_a[0, d], rg_b_i[0, d]]).reshape(1, -1) for d in range(2)]
    P["rg_lam"] = [row(rg_lam[0, d]) for d in range(2)]
    P["ab_w_out"] = ab_w_out[0].astype(BF16)

    w_in = mla_w_in[0]
    base = MLA_Q_RANK + MLA_KV_RANK
    w_kr = w_in[:, base:]
    P["mla_w_in"] = jnp.concatenate(
        [w_in[:, :base], _pad_lanes(w_kr, MLA_NOPE), _pad_lanes(_rot_cols(w_kr), MLA_NOPE)],
        axis=1).astype(BF16)
    P["mla_q_norm"] = row(mla_q_norm[0])
    P["mla_kv_norm"] = row(mla_kv_norm[0])
    wq = mla_w_qb[0].reshape(MLA_Q_RANK, MLA_HEADS, MLA_QK)
    wq_plain = _pad_lanes(wq, 0)
    wq_rot = _pad_lanes(_rot_cols(wq[..., MLA_NOPE:]), MLA_NOPE)
    P["mla_wq"] = jnp.concatenate(
        [wq_plain.reshape(MLA_Q_RANK, -1), wq_rot.reshape(MLA_Q_RANK, -1)], axis=1).astype(BF16)
    wkv = mla_w_kvb[0].reshape(MLA_KV_RANK, MLA_HEADS, MLA_NOPE + MLA_V)
    P["mla_wk"] = _pad_lanes(wkv[..., :MLA_NOPE], 0).reshape(MLA_KV_RANK, -1).astype(BF16)
    P["mla_wvt"] = wkv[..., MLA_NOPE:].reshape(MLA_KV_RANK, -1).T.astype(BF16)
    P["mla_w_out"] = mla_w_out[0].astype(BF16)
    P["ffn"] = [(ffn_w_gate[l].astype(BF16), ffn_w_up[l].astype(BF16),
                 ffn_w_down[l].astype(BF16)) for l in range(2)]
    return P


def _rope_slabs(S):
    inv_freq = 1.0 / (ROPE_THETA ** (jnp.arange(0, MLA_ROPE, 2, dtype=F32) / MLA_ROPE))
    ang = jnp.arange(S, dtype=F32)[:, None] * inv_freq[None, :]
    cos, sin = jnp.cos(ang), jnp.sin(ang)
    pad = LANES - MLA_QK
    cos_tab = jnp.concatenate([jnp.ones((S, MLA_NOPE), F32), cos, cos, jnp.zeros((S, pad), F32)], 1)
    sin_tab = jnp.concatenate([jnp.zeros((S, MLA_NOPE), F32), sin, sin, jnp.zeros((S, pad), F32)], 1)
    return cos_tab, sin_tab


def _trunk(x, P):
    S = x.shape[1]
    xg, qkv = _ab_in_proj(x, P["norm_mix"][0], P["ab_w_in"])
    hf = _rglru_pass(xg, None, P["conv_w"], P["conv_b"], P["rg_w"][0], P["rg_b"][0],
                     P["rg_lam"][0], reverse=False)
    y_rnn = _rglru_pass(xg, hf, P["conv_w"], P["conv_b"], P["rg_w"][1], P["rg_b"][1],
                        P["rg_lam"][1], reverse=True)
    o = _dilated_attention(qkv)
    x = _out_ffn(x, [y_rnn, o], P["ab_w_out"], P["norm_ffn"][0], *P["ffn"][0])
    cos_tab, sin_tab = _rope_slabs(S)
    q, k, vt = _mla_in_proj(x, P["norm_mix"][1], P["mla_w_in"], P["mla_q_norm"], P["mla_kv_norm"],
                            P["mla_wq"], P["mla_wk"], P["mla_wvt"], cos_tab, sin_tab)
    o = _mla_attention(q, k, vt)
    return _out_ffn(x, [o], P["mla_w_out"], P["norm_ffn"][1], *P["ffn"][1],
                    g_final=P["norm_final"])


def kernel(x_prompt, x_sample, norm_mix, norm_ffn, norm_final, ab_w_in, ab_conv_w, ab_conv_b, rg_w_a, rg_b_a, rg_w_i, rg_b_i, rg_lam, ab_w_out, mla_w_in, mla_q_norm, mla_w_qb, mla_kv_norm, mla_w_kvb, mla_w_out, ffn_w_gate, ffn_w_up, ffn_w_down):
    P = _prepare(norm_mix, norm_ffn, norm_final, ab_w_in, ab_conv_w, ab_conv_b, rg_w_a, rg_b_a,
                 rg_w_i, rg_b_i, rg_lam, ab_w_out, mla_w_in, mla_q_norm, mla_w_qb, mla_kv_norm,
                 mla_w_kvb, mla_w_out, ffn_w_gate, ffn_w_up, ffn_w_down)
    return (_trunk(x_prompt, P), _trunk(x_sample, P))
```

```python
import functools
import math

import jax
import jax.numpy as jnp
from jax import lax
from jax.experimental import pallas as pl
from jax.experimental.pallas import tpu as pltpu

F32 = jnp.float32
BF16 = jnp.bfloat16

D_MODEL = 1024
EPS = 1e-6
RG_WIDTH = 512
RG_BLOCKS = 8
RG_C = 8.0
DA_HEADS = 8
DA_HEAD_DIM = 64
DA_WIDTH = 512
DA_DILATIONS = (1, 4, 16)
DA_HALF_STEPS = 64
DA_TOKENS = DA_HALF_STEPS * max(DA_DILATIONS)
MLA_HEADS = 16
MLA_Q_RANK = 384
MLA_KV_RANK = 256
MLA_NOPE = 64
MLA_ROPE = 32
MLA_V = 64
MLA_QK = MLA_NOPE + MLA_ROPE
MLA_ONES = 16
MLA_VROWS = MLA_V + MLA_ONES
ROPE_THETA = 10000.0
FFN_HIDDEN = 2816
NEG_BIG = -1e30
LANES = 128

VMEM_LIMIT = 56 * 1024 * 1024

ROW_TILE = 512
SCAN_TILE = 512
MLA_Q_TILE = 512
MLA_K_TILE = ROW_TILE


def _params(*sem):
    return pltpu.CompilerParams(dimension_semantics=sem, vmem_limit_bytes=VMEM_LIMIT)


def _const_spec(shape):
    nd = len(shape)
    return pl.BlockSpec(shape, lambda *_: (0,) * nd)


def _rms(x, g):
    return x * lax.rsqrt(jnp.mean(x * x, axis=-1, keepdims=True) + EPS) * g


def _ab_in_kernel(x_ref, g_ref, w_ref, xg_ref, qkv_ref):
    h = _rms(x_ref[...], g_ref[...]).astype(BF16)
    xg_ref[...] = jnp.dot(h, w_ref[:, :2 * RG_WIDTH], preferred_element_type=F32)
    n_slabs = 3 * DA_WIDTH // LANES
    qkv = jnp.dot(h, w_ref[:, 2 * RG_WIDTH:], preferred_element_type=F32)
    for j in range(n_slabs):
        qkv_ref[j] = qkv[:, LANES * j:LANES * (j + 1)]


def _ab_in_proj(x, g, w):
    B, S, D = x.shape
    n_slabs = 3 * DA_WIDTH // LANES
    tm = ROW_TILE
    return pl.pallas_call(
        _ab_in_kernel,
        grid=(B, S // tm),
        in_specs=[
            pl.BlockSpec((None, tm, D), lambda b, i: (b, i, 0)),
            _const_spec(g.shape),
            _const_spec(w.shape),
        ],
        out_specs=[
            pl.BlockSpec((None, tm, 2 * RG_WIDTH), lambda b, i: (b, i, 0)),
            pl.BlockSpec((None, n_slabs, tm, LANES), lambda b, i: (b, 0, i, 0)),
        ],
        out_shape=[
            jax.ShapeDtypeStruct((B, S, 2 * RG_WIDTH), F32),
            jax.ShapeDtypeStruct((B, n_slabs, S, LANES), F32),
        ],
        compiler_params=_params("parallel", "parallel"),
        name="ab_in_proj",
    )(x, g, w)


def _rglru_kernel(reverse, n_chunks, *refs):
    if reverse:
        (x_ref, xp_ref, xn_ref, gate_ref, hf_ref, cw_ref, cb_ref, wg_ref, bg_ref, lam_ref,
         out_ref, carry_ref) = refs
    else:
        (x_ref, xp_ref, xn_ref, cw_ref, cb_ref, wg_ref, bg_ref, lam_ref,
         out_ref, carry_ref) = refs
    i = pl.program_id(1)
    c = (n_chunks - 1 - i) if reverse else i
    tc, width = x_ref.shape

    @pl.when(i == 0)
    def _():
        carry_ref[...] = jnp.zeros_like(carry_ref)

    x = x_ref[...]
    row = lax.broadcasted_iota(jnp.int32, (tc, width), 0)
    prev = jnp.where(c > 0, xp_ref[...], 0.0)
    nxt = jnp.where(c < n_chunks - 1, xn_ref[...], 0.0)
    p7, p6, n0 = prev[7:8], prev[6:7], nxt[0:1]
    xm1 = jnp.where(row == 0, p7, pltpu.roll(x, 1, 0))
    xm2 = jnp.where(row == 0, p6, jnp.where(row == 1, p7, pltpu.roll(x, 2, 0)))
    xp1 = jnp.where(row == tc - 1, n0, pltpu.roll(x, tc - 1, 0))
    cw = cw_ref[...]
    xc = cw[0:1] * xm2 + cw[1:2] * xm1 + cw[2:3] * x + cw[3:4] * xp1 + cb_ref[...]

    z = jnp.dot(xc.astype(BF16), wg_ref[...], preferred_element_type=F32) + bg_ref[...]
    r = jax.nn.sigmoid(z[:, :width])
    gi = jax.nn.sigmoid(z[:, width:])
    log_a = (-RG_C) * r * jax.nn.softplus(-lam_ref[...])
    a = jnp.exp(log_a)
    u = jnp.sqrt(-jnp.tanh(log_a) * (a * a + 1.0)) * (gi * xc)

    d = 1
    while d < tc:
        shift = (tc - d) if reverse else d
        valid = (row < tc - d) if reverse else (row >= d)
        a_sh = pltpu.roll(a, shift, 0)
        u_sh = pltpu.roll(u, shift, 0)
        u = jnp.where(valid, a * u_sh + u, u)
        a = jnp.where(valid, a * a_sh, a)
        d *= 2
    h = u + a * carry_ref[...]
    carry_ref[...] = h[0:1] if reverse else h[tc - 1:tc]

    if reverse:
        out_ref[...] = (jax.nn.gelu(gate_ref[...]) * (hf_ref[...] + h)).astype(out_ref.dtype)
    else:
        out_ref[...] = h


def _rglru_pass(xg, hf, conv_w, conv_b, wg, bg, lam, reverse):
    B, S, _ = xg.shape
    tc = SCAN_TILE
    n_chunks = S // tc
    sub = 8

    def chunk(i):
        return (n_chunks - 1 - i) if reverse else i

    x_spec = pl.BlockSpec((None, tc, RG_WIDTH), lambda b, i: (b, chunk(i), 0))
    prev_spec = pl.BlockSpec(
        (None, sub, RG_WIDTH), lambda b, i: (b, jnp.maximum(chunk(i) * (tc // sub) - 1, 0), 0))
    next_spec = pl.BlockSpec(
        (None, sub, RG_WIDTH),
        lambda b, i: (b, jnp.minimum((chunk(i) + 1) * (tc // sub), S // sub - 1), 0))
    consts = [conv_w, conv_b, wg, bg, lam]
    const_specs = [_const_spec(t.shape) for t in consts]
    if reverse:
        gate_spec = pl.BlockSpec((None, tc, RG_WIDTH), lambda b, i: (b, chunk(i), 1))
        hf_spec = pl.BlockSpec((None, tc, RG_WIDTH), lambda b, i: (b, chunk(i), 0))
        inputs = [xg, xg, xg, xg, hf] + consts
        in_specs = [x_spec, prev_spec, next_spec, gate_spec, hf_spec] + const_specs
        out_dtype = BF16
    else:
        inputs = [xg, xg, xg] + consts
        in_specs = [x_spec, prev_spec, next_spec] + const_specs
        out_dtype = F32
    return pl.pallas_call(
        functools.partial(_rglru_kernel, reverse, n_chunks),
        grid=(B, n_chunks),
        in_specs=in_specs,
        out_specs=pl.BlockSpec((None, tc, RG_WIDTH), lambda b, i: (b, chunk(i), 0)),
        out_shape=jax.ShapeDtypeStruct((B, S, RG_WIDTH), out_dtype),
        scratch_shapes=[pltpu.VMEM((1, RG_WIDTH), F32)],
        compiler_params=_params("parallel", "arbitrary"),
        name="rglru_bwd" if reverse else "rglru_fwd",
    )(*inputs)


def _dilated_kernel(n_blocks, slopes, q_ref, kp_ref, kc_ref, kn_ref, vp_ref, vc_ref, vn_ref,
                    out_ref, qs_ref, ks_ref, vs_ref, o_scr, lse_scr, bias_scr):
    i = pl.program_id(1)
    T = DA_TOKENS
    W = DA_HALF_STEPS
    n_pairs = DA_WIDTH // LANES
    log2e = math.log2(math.e)
    scale = log2e / math.sqrt(DA_HEAD_DIM)
    is_first = i == 0
    is_last = i == n_blocks - 1

    for g, dil in enumerate(DA_DILATIONS):
        n = T // dil
        bq = min(n, 128)
        bk = bq + 2 * W
        rowq = lax.broadcasted_iota(jnp.int32, (bq, bk), 0)
        colk = lax.broadcasted_iota(jnp.int32, (bq, bk), 1)
        rel = colk - W - rowq
        band = jnp.abs(rel) <= W
        neg_dist = -(jnp.abs(rel) * dil).astype(F32)
        lane = lax.broadcasted_iota(jnp.int32, (bq, LANES), 1)
        lo_lane = lane < DA_HEAD_DIM
        for head in range(DA_HEADS):
            bias_scr[head, 0:bq, 0:bk] = jnp.where(band, (slopes[head] * log2e) * neg_dist,
                                                   NEG_BIG)

        def class_body(r, carry, dil=dil, n=n, bq=bq, bk=bk, g=g, colk=colk,
                       lo_lane=lo_lane):
            for p in range(n_pairs):
                qs_ref[p, 0:n] = (q_ref[p, pl.ds(r, n, stride=dil), :] * scale).astype(BF16)
                for src_p, src_c, src_n, dst in ((kp_ref, kc_ref, kn_ref, ks_ref),
                                                 (vp_ref, vc_ref, vn_ref, vs_ref)):
                    dst[p, 0:W] = src_p[p, pl.ds(r + dil * (n - W), W, stride=dil), :].astype(BF16)
                    dst[p, W:W + n] = src_c[p, pl.ds(r, n, stride=dil), :].astype(BF16)
                    dst[p, W + n:2 * W + n] = src_n[p, pl.ds(r, W, stride=dil), :].astype(BF16)

            def qb_body(qb, carry2):
                q0 = pl.multiple_of(qb * bq, bq)
                pos = q0 + colk
                in_seq = jnp.logical_not(
                    jnp.logical_or(jnp.logical_and(is_first, pos < W),
                                   jnp.logical_and(is_last, pos >= n + W)))
                scores = []
                for p in range(n_pairs):
                    qp = qs_ref[p, pl.ds(q0, bq), :]
                    kp = ks_ref[p, pl.ds(q0, bk), :]
                    for hh in range(2):
                        qh = jnp.where(lo_lane if hh == 0 else jnp.logical_not(lo_lane), qp,
                                       jnp.zeros_like(qp))
                        scores.append(lax.dot_general(qh, kp, (((1,), (1,)), ((), ())),
                                                      preferred_element_type=F32))
                probs, lse2, inv_l = [], [], []
                for head, s in enumerate(scores):
                    s = jnp.where(in_seq, s + bias_scr[head, 0:bq, 0:bk], NEG_BIG)
                    m = jnp.max(s, axis=-1, keepdims=True)
                    e = jnp.exp2(s - m)
                    l = jnp.sum(e, axis=-1, keepdims=True)
                    probs.append(e.astype(BF16))
                    inv_l.append(1.0 / l)
                    lse2.append(m + jnp.log2(l))
                for p in range(n_pairs):
                    vp = vs_ref[p, pl.ds(q0, bk), :]
                    o_h = [jnp.dot(probs[2 * p + hh], vp, preferred_element_type=F32)
                           * inv_l[2 * p + hh] for hh in range(2)]
                    rows = pl.ds(r + dil * q0, bq, stride=dil)
                    o_scr[g, p, rows, :] = jnp.where(lo_lane, o_h[0], o_h[1])
                    lse_scr[g, p, rows, :] = jnp.where(lo_lane, lse2[2 * p], lse2[2 * p + 1])
                return carry2

            lax.fori_loop(0, n // bq, qb_body, 0)
            return carry

        lax.fori_loop(0, dil, class_body, 0)

    rows_per_step = 256

    def merge_body(t, carry):
        t0 = pl.multiple_of(t * rows_per_step, rows_per_step)
        rows = pl.ds(t0, rows_per_step)
        for p in range(n_pairs):
            lse = [lse_scr[g, p, rows, :] for g in range(len(DA_DILATIONS))]
            mx = jnp.maximum(jnp.maximum(lse[0], lse[1]), lse[2])
            w = [jnp.exp2(v - mx) for v in lse]
            num = sum(w[g] * o_scr[g, p, rows, :] for g in range(len(DA_DILATIONS)))
            out_ref[rows, LANES * p:LANES * (p + 1)] = (num / (w[0] + w[1] + w[2])).astype(
                out_ref.dtype)
        return carry

    lax.fori_loop(0, T // rows_per_step, merge_body, 0)


def _dilated_attention(qkv):
    B, _, S, _ = qkv.shape
    T = DA_TOKENS
    n_blocks = S // T
    n_pairs = DA_WIDTH // LANES
    slopes = tuple(2.0 ** (-8.0 * (h + 1) / DA_HEADS) for h in range(DA_HEADS))

    def spec(slab, shift):
        def index(b, i):
            return (b, slab, jnp.clip(i + shift, 0, n_blocks - 1), 0)
        return pl.BlockSpec((None, n_pairs, T, LANES), index)

    win = T + 2 * DA_HALF_STEPS
    return pl.pallas_call(
        functools.partial(_dilated_kernel, n_blocks, slopes),
        grid=(B, n_blocks),
        in_specs=[spec(0, 0), spec(1, -1), spec(1, 0), spec(1, 1),
                  spec(2, -1), spec(2, 0), spec(2, 1)],
        out_specs=pl.BlockSpec((None, T, DA_WIDTH), lambda b, i: (b, i, 0)),
        out_shape=jax.ShapeDtypeStruct((B, S, DA_WIDTH), BF16),
        scratch_shapes=[
            pltpu.VMEM((n_pairs, T, LANES), BF16),
            pltpu.VMEM((n_pairs, win, LANES), BF16),
            pltpu.VMEM((n_pairs, win, LANES), BF16),
            pltpu.VMEM((len(DA_DILATIONS), n_pairs, T, LANES), F32),
            pltpu.VMEM((len(DA_DILATIONS), n_pairs, T, LANES), F32),
            pltpu.VMEM((DA_HEADS, 128, 128 + 2 * DA_HALF_STEPS), F32),
        ],
        compiler_params=_params("parallel", "parallel"),
        name="dilated_attention",
    )(qkv, qkv, qkv, qkv, qkv, qkv, qkv)


def _out_ffn_kernel(n_y, final_norm, *refs):
    x_ref = refs[0]
    y_refs = refs[1:1 + n_y]
    wo_ref, g_ref, wg_ref, wu_ref, wd_ref = refs[1 + n_y:6 + n_y]
    gf_ref = refs[6 + n_y] if final_norm else None
    out_ref = refs[-1]

    y = y_refs[0][...] if n_y == 1 else jnp.concatenate([r[...] for r in y_refs], axis=1)
    x = x_ref[...] + jnp.dot(y, wo_ref[...], preferred_element_type=F32)
    h = _rms(x, g_ref[...]).astype(BF16)
    gate = jnp.dot(h, wg_ref[...], preferred_element_type=F32)
    up = jnp.dot(h, wu_ref[...], preferred_element_type=F32)
    act = (jax.nn.silu(gate) * up).astype(BF16)
    x = x + jnp.dot(act, wd_ref[...], preferred_element_type=F32)
    if final_norm:
        x = _rms(x, gf_ref[...])
    out_ref[...] = x


def _out_ffn(x, ys, w_out, g, wg, wu, wd, g_final=None):
    B, S, D = x.shape
    rows = B * S
    tm = ROW_TILE
    x2 = x.reshape(rows, D)
    ys2 = [y.reshape(rows, y.shape[-1]) for y in ys]
    consts = [w_out, g, wg, wu, wd] + ([g_final] if g_final is not None else [])
    out = pl.pallas_call(
        functools.partial(_out_ffn_kernel, len(ys), g_final is not None),
        grid=(rows // tm,),
        in_specs=([pl.BlockSpec((tm, D), lambda i: (i, 0))]
                  + [pl.BlockSpec((tm, y.shape[-1]), lambda i: (i, 0)) for y in ys2]
                  + [pl.BlockSpec(t.shape, lambda i, nd=t.ndim: (0,) * nd,
                                  pipeline_mode=pl.Buffered(1)) for t in consts]),
        out_specs=pl.BlockSpec((tm, D), lambda i: (i, 0)),
        out_shape=jax.ShapeDtypeStruct((rows, D), F32),
        compiler_params=_params("parallel"),
        name="out_proj_ffn",
    )(x2, *ys2, *consts)
    return out.reshape(B, S, D)


def _mla_in_kernel(x_ref, g_ref, win_ref, qn_ref, kvn_ref, wq_ref, wk_ref, wvt_ref, cos_ref,
                   sin_ref, q_ref, k_ref, vt_ref):
    h = _rms(x_ref[...], g_ref[...]).astype(BF16)
    proj = jnp.dot(h, win_ref[...], preferred_element_type=F32)
    cq = _rms(proj[:, :MLA_Q_RANK], qn_ref[...]).astype(BF16)
    ckv = _rms(proj[:, MLA_Q_RANK:MLA_Q_RANK + MLA_KV_RANK], kvn_ref[...]).astype(BF16)
    base = MLA_Q_RANK + MLA_KV_RANK
    cos = cos_ref[...]
    sin = sin_ref[...]
    k_rope = proj[:, base:base + LANES] * cos + proj[:, base + LANES:base + 2 * LANES] * sin
    scale = math.log2(math.e) / math.sqrt(MLA_QK)
    width = MLA_HEADS * LANES
    for hd in range(MLA_HEADS):
        cols = slice(hd * LANES, (hd + 1) * LANES)
        rot_cols = slice(width + hd * LANES, width + (hd + 1) * LANES)
        qa = jnp.dot(cq, wq_ref[:, cols], preferred_element_type=F32)
        qb = jnp.dot(cq, wq_ref[:, rot_cols], preferred_element_type=F32)
        q_ref[:, cols] = ((qa * cos + qb * sin) * scale).astype(BF16)
        kn = jnp.dot(ckv, wk_ref[:, cols], preferred_element_type=F32)
        k_ref[:, cols] = (kn + k_rope).astype(BF16)
    vt = lax.dot_general(wvt_ref[...], ckv, (((1,), (1,)), ((), ())),
                         preferred_element_type=F32)
    ones = jnp.ones((MLA_ONES, vt.shape[1]), BF16)
    for hd in range(MLA_HEADS):
        vt_ref[hd * MLA_VROWS:hd * MLA_VROWS + MLA_V, :] = (
            vt[hd * MLA_V:(hd + 1) * MLA_V].astype(BF16))
        vt_ref[hd * MLA_VROWS + MLA_V:(hd + 1) * MLA_VROWS, :] = ones


def _mla_in_proj(x, g, win, qn, kvn, wq, wk, wvt, cos_tab, sin_tab):
    B, S, D = x.shape
    tm = ROW_TILE
    width = MLA_HEADS * LANES
    consts = [g, win, qn, kvn, wq, wk, wvt]
    return pl.pallas_call(
        _mla_in_kernel,
        grid=(B, S // tm),
        in_specs=([pl.BlockSpec((None, tm, D), lambda b, i: (b, i, 0))]
                  + [_const_spec(t.shape) for t in consts]
                  + [pl.BlockSpec((tm, LANES), lambda b, i: (i, 0)),
                     pl.BlockSpec((tm, LANES), lambda b, i: (i, 0))]),
        out_specs=[
            pl.BlockSpec((None, tm, width), lambda b, i: (b, i, 0)),
            pl.BlockSpec((None, tm, width), lambda b, i: (b, i, 0)),
            pl.BlockSpec((None, None, MLA_HEADS * MLA_VROWS, tm), lambda b, i: (b, i, 0, 0)),
        ],
        out_shape=[
            jax.ShapeDtypeStruct((B, S, width), BF16),
            jax.ShapeDtypeStruct((B, S, width), BF16),
            jax.ShapeDtypeStruct((B, S // tm, MLA_HEADS * MLA_VROWS, tm), BF16),
        ],
        compiler_params=_params("parallel", "parallel"),
        name="mla_in_proj",
    )(x, *consts, cos_tab, sin_tab)


def _mla_attn_kernel(q_ref, k_ref, vt_ref, o_ref, acc_scr):
    n_kb, _, bk = vt_ref.shape
    bq = q_ref.shape[0]
    qs = [q_ref[:, hh * LANES:(hh + 1) * LANES] for hh in range(2)]

    rows = vt_ref.shape[1] // 2

    def scores(j, hh):
        return lax.dot_general(k_ref[j * bk:(j + 1) * bk, hh * LANES:(hh + 1) * LANES], qs[hh],
                               (((1,), (1,)), ((), ())), preferred_element_type=F32)

    def update(j, hh, m, st):
        m_new = jnp.maximum(m, jnp.max(st, axis=0, keepdims=True))
        alpha = jnp.exp2(m - m_new)
        chunk = 16
        p = jnp.concatenate(
            [jnp.exp2(st[r:r + chunk] - m_new).astype(BF16) for r in range(0, bk, chunk)], axis=0)
        vb = vt_ref[j, hh * rows:(hh + 1) * rows, :]
        acc_scr[hh] = alpha * acc_scr[hh] + jnp.dot(vb, p, preferred_element_type=F32)
        return m_new

    acc_scr[...] = jnp.zeros_like(acc_scr)
    ms = [jnp.full((1, bq), NEG_BIG, F32) for _ in range(2)]
    sts = [scores(0, hh) for hh in range(2)]
    for j in range(n_kb):
        next_sts = [scores(j + 1, hh) for hh in range(2)] if j + 1 < n_kb else None
        ms = [update(j, hh, ms[hh], sts[hh]) for hh in range(2)]
        sts = next_sts
    outs = [acc_scr[hh, 0:MLA_V] / acc_scr[hh, MLA_V:MLA_V + 1] for hh in range(2)]
    o_ref[...] = jnp.concatenate(outs, axis=0).T.astype(o_ref.dtype)


def _mla_attention(q, k, vt):
    B, S, _ = q.shape
    n_kb = vt.shape[1]
    bq = MLA_Q_TILE
    n_pairs = MLA_HEADS // 2
    return pl.pallas_call(
        _mla_attn_kernel,
        grid=(B, n_pairs, S // bq),
        in_specs=[
            pl.BlockSpec((None, bq, 2 * LANES), lambda b, p, i: (b, i, p)),
            pl.BlockSpec((None, S, 2 * LANES), lambda b, p, i: (b, 0, p)),
            pl.BlockSpec((None, n_kb, 2 * MLA_VROWS, MLA_K_TILE), lambda b, p, i: (b, 0, p, 0)),
        ],
        out_specs=pl.BlockSpec((None, bq, 2 * MLA_V), lambda b, p, i: (b, i, p)),
        out_shape=jax.ShapeDtypeStruct((B, S, MLA_HEADS * MLA_V), BF16),
        scratch_shapes=[
            pltpu.VMEM((2, MLA_VROWS, bq), F32),
        ],
        compiler_params=_params("parallel", "parallel", "arbitrary"),
        name="mla_attention",
    )(q, k, vt)


def _block_diag(w):
    nb, d, _ = w.shape
    eye = jnp.eye(nb, dtype=w.dtype)
    return (eye[:, None, :, None] * w[:, :, None, :]).reshape(nb * d, nb * d)


def _rot_cols(w):
    half = MLA_ROPE // 2
    return jnp.concatenate([-w[..., half:], w[..., :half]], axis=-1)


def _pad_lanes(w, offset):
    return jnp.pad(w, [(0, 0)] * (w.ndim - 1) + [(offset, LANES - offset - w.shape[-1])])


def _prepare(norm_mix, norm_ffn, norm_final, ab_w_in, ab_conv_w, ab_conv_b, rg_w_a, rg_b_a,
             rg_w_i, rg_b_i, rg_lam, ab_w_out, mla_w_in, mla_q_norm, mla_w_qb, mla_kv_norm,
             mla_w_kvb, mla_w_out, ffn_w_gate, ffn_w_up, ffn_w_down):
    row = lambda v: v.reshape(1, -1).astype(F32)
    P = {}
    P["norm_mix"] = [row(norm_mix[l]) for l in range(2)]
    P["norm_ffn"] = [row(norm_ffn[l]) for l in range(2)]
    P["norm_final"] = row(norm_final)
    P["ab_w_in"] = ab_w_in[0].astype(BF16)
    P["conv_w"] = ab_conv_w[0].astype(F32)
    P["conv_b"] = row(ab_conv_b[0])
    P["rg_w"] = [jnp.concatenate([_block_diag(rg_w_a[0, d]), _block_diag(rg_w_i[0, d])],
                                 axis=1).astype(BF16) for d in range(2)]
    P["rg_b"] = [jnp.concatenate([rg_b_a[0, d], rg_b_i[0, d]]).reshape(1, -1) for d in range(2)]
    P["rg_lam"] = [row(rg_lam[0, d]) for d in range(2)]
    P["ab_w_out"] = ab_w_out[0].astype(BF16)

    w_in = mla_w_in[0]
    base = MLA_Q_RANK + MLA_KV_RANK
    w_kr = w_in[:, base:]
    P["mla_w_in"] = jnp.concatenate(
        [w_in[:, :base], _pad_lanes(w_kr, MLA_NOPE), _pad_lanes(_rot_cols(w_kr), MLA_NOPE)],
        axis=1).astype(BF16)
    P["mla_q_norm"] = row(mla_q_norm[0])
    P["mla_kv_norm"] = row(mla_kv_norm[0])
    wq = mla_w_qb[0].reshape(MLA_Q_RANK, MLA_HEADS, MLA_QK)
    wq_plain = _pad_lanes(wq, 0)
    wq_rot = _pad_lanes(_rot_cols(wq[..., MLA_NOPE:]), MLA_NOPE)
    P["mla_wq"] = jnp.concatenate(
        [wq_plain.reshape(MLA_Q_RANK, -1), wq_rot.reshape(MLA_Q_RANK, -1)], axis=1).astype(BF16)
    wkv = mla_w_kvb[0].reshape(MLA_KV_RANK, MLA_HEADS, MLA_NOPE + MLA_V)
    P["mla_wk"] = _pad_lanes(wkv[..., :MLA_NOPE], 0).reshape(MLA_KV_RANK, -1).astype(BF16)
    P["mla_wvt"] = wkv[..., MLA_NOPE:].reshape(MLA_KV_RANK, -1).T.astype(BF16)
    P["mla_w_out"] = mla_w_out[0].astype(BF16)
    P["ffn"] = [(ffn_w_gate[l].astype(BF16), ffn_w_up[l].astype(BF16),
                 ffn_w_down[l].astype(BF16)) for l in range(2)]
    return P


def _rope_slabs(S):
    inv_freq = 1.0 / (ROPE_THETA ** (jnp.arange(0, MLA_ROPE, 2, dtype=F32) / MLA_ROPE))
    ang = jnp.arange(S, dtype=F32)[:, None] * inv_freq[None, :]
    cos, sin = jnp.cos(ang), jnp.sin(ang)
    pad = LANES - MLA_QK
    cos_tab = jnp.concatenate([jnp.ones((S, MLA_NOPE), F32), cos, cos, jnp.zeros((S, pad), F32)], 1)
    sin_tab = jnp.concatenate([jnp.zeros((S, MLA_NOPE), F32), sin, sin, jnp.zeros((S, pad), F32)], 1)
    return cos_tab, sin_tab


def _trunk(x, P):
    S = x.shape[1]
    xg, qkv = _ab_in_proj(x, P["norm_mix"][0], P["ab_w_in"])
    hf = _rglru_pass(xg, None, P["conv_w"], P["conv_b"], P["rg_w"][0], P["rg_b"][0],
                     P["rg_lam"][0], reverse=False)
    y_rnn = _rglru_pass(xg, hf, P["conv_w"], P["conv_b"], P["rg_w"][1], P["rg_b"][1],
                        P["rg_lam"][1], reverse=True)
    o = _dilated_attention(qkv)
    x = _out_ffn(x, [y_rnn, o], P["ab_w_out"], P["norm_ffn"][0], *P["ffn"][0])
    cos_tab, sin_tab = _rope_slabs(S)
    q, k, vt = _mla_in_proj(x, P["norm_mix"][1], P["mla_w_in"], P["mla_q_norm"], P["mla_kv_norm"],
                            P["mla_wq"], P["mla_wk"], P["mla_wvt"], cos_tab, sin_tab)
    o = _mla_attention(q, k, vt)
    return _out_ffn(x, [o], P["mla_w_out"], P["norm_ffn"][1], *P["ffn"][1],
                    g_final=P["norm_final"])


def kernel(x_prompt, x_sample, norm_mix, norm_ffn, norm_final, ab_w_in, ab_conv_w, ab_conv_b, rg_w_a, rg_b_a, rg_w_i, rg_b_i, rg_lam, ab_w_out, mla_w_in, mla_q_norm, mla_w_qb, mla_kv_norm, mla_w_kvb, mla_w_out, ffn_w_gate, ffn_w_up, ffn_w_down):
    P = _prepare(norm_mix, norm_ffn, norm_final, ab_w_in, ab_conv_w, ab_conv_b, rg_w_a, rg_b_a,
                 rg_w_i, rg_b_i, rg_lam, ab_w_out, mla_w_in, mla_q_norm, mla_w_qb, mla_kv_norm,
                 mla_w_kvb, mla_w_out, ffn_w_gate, ffn_w_up, ffn_w_down)
    return (_trunk(x_prompt, P), _trunk(x_sample, P))
```

```python
import functools
import math

import jax
import jax.numpy as jnp
from jax import lax
from jax.experimental import pallas as pl
from jax.experimental.pallas import tpu as pltpu

F32 = jnp.float32
BF16 = jnp.bfloat16

D_MODEL = 1024
EPS = 1e-6
RG_WIDTH = 512
RG_BLOCKS = 8
RG_C = 8.0
DA_HEADS = 8
DA_HEAD_DIM = 64
DA_WIDTH = 512
DA_DILATIONS = (1, 4, 16)
DA_HALF_STEPS = 64
DA_TOKENS = DA_HALF_STEPS * max(DA_DILATIONS)
MLA_HEADS = 16
MLA_Q_RANK = 384
MLA_KV_RANK = 256
MLA_NOPE = 64
MLA_ROPE = 32
MLA_V = 64
MLA_QK = MLA_NOPE + MLA_ROPE
MLA_ONES = 16
MLA_VROWS = MLA_V + MLA_ONES
ROPE_THETA = 10000.0
FFN_HIDDEN = 2816
NEG_BIG = -1e30
LANES = 128

VMEM_LIMIT = 56 * 1024 * 1024

ROW_TILE = 512
SCAN_TILE = 512
SCAN_PAD = 8
MLA_Q_TILE = 1024
MLA_K_TILE = ROW_TILE


def _params(*sem):
    return pltpu.CompilerParams(dimension_semantics=sem, vmem_limit_bytes=VMEM_LIMIT)


def _const_spec(shape):
    nd = len(shape)
    return pl.BlockSpec(shape, lambda *_: (0,) * nd)


def _rms(x, g):
    return x * lax.rsqrt(jnp.mean(x * x, axis=-1, keepdims=True) + EPS) * g


def _ab_in_kernel(x_ref, g_ref, w_ref, xg_ref, qkv_ref):
    h = _rms(x_ref[...], g_ref[...]).astype(BF16)
    xg_ref[...] = jnp.dot(h, w_ref[:, :2 * RG_WIDTH], preferred_element_type=F32)
    n_slabs = 3 * DA_WIDTH // LANES
    qkv = jnp.dot(h, w_ref[:, 2 * RG_WIDTH:], preferred_element_type=F32)
    for j in range(n_slabs):
        qkv_ref[j] = qkv[:, LANES * j:LANES * (j + 1)]


def _ab_in_proj(x, g, w):
    B, S, D = x.shape
    n_slabs = 3 * DA_WIDTH // LANES
    tm = ROW_TILE
    return pl.pallas_call(
        _ab_in_kernel,
        grid=(B, S // tm),
        in_specs=[
            pl.BlockSpec((None, tm, D), lambda b, i: (b, i, 0)),
            _const_spec(g.shape),
            _const_spec(w.shape),
        ],
        out_specs=[
            pl.BlockSpec((None, tm, 2 * RG_WIDTH), lambda b, i: (b, i, 0)),
            pl.BlockSpec((None, n_slabs, tm, LANES), lambda b, i: (b, 0, i, 0)),
        ],
        out_shape=[
            jax.ShapeDtypeStruct((B, S, 2 * RG_WIDTH), F32),
            jax.ShapeDtypeStruct((B, n_slabs, S, LANES), F32),
        ],
        compiler_params=_params("parallel", "parallel"),
        name="ab_in_proj",
    )(x, g, w)


def _shift_rows(x, d, edge, toward_end):
    t = SCAN_PAD
    n = x.shape[0] // t
    sub = lax.broadcasted_iota(jnp.int32, (t, x.shape[1]), 0)
    tiles = [x[t * i:t * (i + 1)] for i in range(n)]
    if toward_end:
        rot = [pltpu.roll(b, d, 0) for b in [edge] + tiles]
        out = [jnp.where(sub < d, rot[i], rot[i + 1]) for i in range(n)]
    else:
        rot = [pltpu.roll(b, t - d, 0) for b in tiles + [edge]]
        out = [jnp.where(sub >= t - d, rot[i + 1], rot[i]) for i in range(n)]
    return jnp.concatenate(out, axis=0)


def _rglru_kernel(reverse, n_chunks, *refs):
    if reverse:
        (x_ref, xp_ref, xn_ref, gate_ref, hf_ref, cw_ref, cb_ref, wg_ref, bg_ref, lam_ref,
         out_ref, carry_ref) = refs
    else:
        (x_ref, xp_ref, xn_ref, cw_ref, cb_ref, wg_ref, bg_ref, lam_ref,
         out_ref, carry_ref) = refs
    i = pl.program_id(1)
    c = (n_chunks - 1 - i) if reverse else i
    tc, width = x_ref.shape

    @pl.when(i == 0)
    def _():
        carry_ref[...] = jnp.zeros_like(carry_ref)

    x = x_ref[...]
    prev = jnp.where(c > 0, xp_ref[...], 0.0)
    nxt = jnp.where(c < n_chunks - 1, xn_ref[...], 0.0)
    cw = cw_ref[...]
    xc = (cw[0:1] * _shift_rows(x, 2, prev, True) + cw[1:2] * _shift_rows(x, 1, prev, True)
          + cw[2:3] * x + cw[3:4] * _shift_rows(x, 1, nxt, False) + cb_ref[...])

    z = jnp.dot(xc.astype(BF16), wg_ref[...], preferred_element_type=F32) + bg_ref[...]
    r = jax.nn.sigmoid(z[:, :width])
    gi = jax.nn.sigmoid(z[:, width:])
    log_a = (-RG_C) * r * jax.nn.softplus(-lam_ref[...])
    a = jnp.exp(log_a)
    u = jnp.sqrt(-jnp.tanh(log_a) * (a * a + 1.0)) * (gi * xc)

    one_edge = jnp.ones((SCAN_PAD, width), F32)
    zero_edge = jnp.zeros((SCAN_PAD, width), F32)
    d = 1
    while d < tc:
        if d < SCAN_PAD:
            a_sh = _shift_rows(a, d, one_edge, not reverse)
            u = a * _shift_rows(u, d, zero_edge, not reverse) + u
            a = a * a_sh
        elif reverse:
            u = jnp.concatenate([a[:tc - d] * u[d:] + u[:tc - d], u[tc - d:]], axis=0)
            a = jnp.concatenate([a[:tc - d] * a[d:], a[tc - d:]], axis=0)
        else:
            u = jnp.concatenate([u[:d], a[d:] * u[:tc - d] + u[d:]], axis=0)
            a = jnp.concatenate([a[:d], a[d:] * a[:tc - d]], axis=0)
        d *= 2
    h = u + a * carry_ref[...]
    carry_ref[...] = h[0:1] if reverse else h[tc - 1:tc]

    if reverse:
        out_ref[...] = (jax.nn.gelu(gate_ref[...]) * (hf_ref[...] + h)).astype(out_ref.dtype)
    else:
        out_ref[...] = h


def _rglru_pass(xg, hf, conv_w, conv_b, wg, bg, lam, reverse):
    B, S, _ = xg.shape
    tc = SCAN_TILE
    n_chunks = S // tc
    sub = SCAN_PAD

    def chunk(i):
        return (n_chunks - 1 - i) if reverse else i

    x_spec = pl.BlockSpec((None, tc, RG_WIDTH), lambda b, i: (b, chunk(i), 0))
    prev_spec = pl.BlockSpec(
        (None, sub, RG_WIDTH), lambda b, i: (b, jnp.maximum(chunk(i) * (tc // sub) - 1, 0), 0))
    next_spec = pl.BlockSpec(
        (None, sub, RG_WIDTH),
        lambda b, i: (b, jnp.minimum((chunk(i) + 1) * (tc // sub), S // sub - 1), 0))
    consts = [conv_w, conv_b, wg, bg, lam]
    const_specs = [_const_spec(t.shape) for t in consts]
    if reverse:
        gate_spec = pl.BlockSpec((None, tc, RG_WIDTH), lambda b, i: (b, chunk(i), 1))
        hf_spec = pl.BlockSpec((None, tc, RG_WIDTH), lambda b, i: (b, chunk(i), 0))
        inputs = [xg, xg, xg, xg, hf] + consts
        in_specs = [x_spec, prev_spec, next_spec, gate_spec, hf_spec] + const_specs
        out_dtype = BF16
    else:
        inputs = [xg, xg, xg] + consts
        in_specs = [x_spec, prev_spec, next_spec] + const_specs
        out_dtype = F32
    return pl.pallas_call(
        functools.partial(_rglru_kernel, reverse, n_chunks),
        grid=(B, n_chunks),
        in_specs=in_specs,
        out_specs=pl.BlockSpec((None, tc, RG_WIDTH), lambda b, i: (b, chunk(i), 0)),
        out_shape=jax.ShapeDtypeStruct((B, S, RG_WIDTH), out_dtype),
        scratch_shapes=[pltpu.VMEM((1, RG_WIDTH), F32)],
        compiler_params=_params("parallel", "arbitrary"),
        name="rglru_bwd" if reverse else "rglru_fwd",
    )(*inputs)


def _dilated_kernel(n_blocks, slopes, q_ref, kp_ref, kc_ref, kn_ref, vp_ref, vc_ref, vn_ref,
                    out_ref, qs_ref, ks_ref, vs_ref, o_scr, lse_scr, bias_scr):
    i = pl.program_id(1)
    T = DA_TOKENS
    W = DA_HALF_STEPS
    n_pairs = DA_WIDTH // LANES
    log2e = math.log2(math.e)
    scale = log2e / math.sqrt(DA_HEAD_DIM)
    is_first = i == 0
    is_last = i == n_blocks - 1

    for g, dil in enumerate(DA_DILATIONS):
        n = T // dil
        bq = min(n, 128)
        bk = bq + 2 * W
        rowq = lax.broadcasted_iota(jnp.int32, (bq, bk), 0)
        colk = lax.broadcasted_iota(jnp.int32, (bq, bk), 1)
        rel = colk - W - rowq
        band = jnp.abs(rel) <= W
        neg_dist = -(jnp.abs(rel) * dil).astype(F32)
        lane = lax.broadcasted_iota(jnp.int32, (bq, LANES), 1)
        lo_lane = lane < DA_HEAD_DIM
        for head in range(DA_HEADS):
            bias_scr[head, 0:bq, 0:bk] = jnp.where(band, (slopes[head] * log2e) * neg_dist,
                                                   NEG_BIG)

        def class_body(r, carry, dil=dil, n=n, bq=bq, bk=bk, g=g, colk=colk,
                       lo_lane=lo_lane):
            for p in range(n_pairs):
                qs_ref[p, 0:n] = (q_ref[p, pl.ds(r, n, stride=dil), :] * scale).astype(BF16)
                for src_p, src_c, src_n, dst in ((kp_ref, kc_ref, kn_ref, ks_ref),
                                                 (vp_ref, vc_ref, vn_ref, vs_ref)):
                    dst[p, 0:W] = src_p[p, pl.ds(r + dil * (n - W), W, stride=dil), :].astype(BF16)
                    dst[p, W:W + n] = src_c[p, pl.ds(r, n, stride=dil), :].astype(BF16)
                    dst[p, W + n:2 * W + n] = src_n[p, pl.ds(r, W, stride=dil), :].astype(BF16)

            def qb_body(qb, carry2):
                q0 = pl.multiple_of(qb * bq, bq)
                pos = q0 + colk
                in_seq = jnp.logical_not(
                    jnp.logical_or(jnp.logical_and(is_first, pos < W),
                                   jnp.logical_and(is_last, pos >= n + W)))
                scores = []
                for p in range(n_pairs):
                    qp = qs_ref[p, pl.ds(q0, bq), :]
                    kp = ks_ref[p, pl.ds(q0, bk), :]
                    for hh in range(2):
                        qh = jnp.where(lo_lane if hh == 0 else jnp.logical_not(lo_lane), qp,
                                       jnp.zeros_like(qp))
                        scores.append(lax.dot_general(qh, kp, (((1,), (1,)), ((), ())),
                                                      preferred_element_type=F32))
                probs, lse2, inv_l = [], [], []
                for head, s in enumerate(scores):
                    s = jnp.where(in_seq, s + bias_scr[head, 0:bq, 0:bk], NEG_BIG)
                    m = jnp.max(s, axis=-1, keepdims=True)
                    e = jnp.exp2(s - m)
                    l = jnp.sum(e, axis=-1, keepdims=True)
                    probs.append(e.astype(BF16))
                    inv_l.append(1.0 / l)
                    lse2.append(m + jnp.log2(l))
                for p in range(n_pairs):
                    vp = vs_ref[p, pl.ds(q0, bk), :]
                    o_h = [jnp.dot(probs[2 * p + hh], vp, preferred_element_type=F32)
                           * inv_l[2 * p + hh] for hh in range(2)]
                    rows = pl.ds(r + dil * q0, bq, stride=dil)
                    o_scr[g, p, rows, :] = jnp.where(lo_lane, o_h[0], o_h[1])
                    lse_scr[g, p, rows, :] = jnp.where(lo_lane, lse2[2 * p], lse2[2 * p + 1])
                return carry2

            lax.fori_loop(0, n // bq, qb_body, 0)
            return carry

        lax.fori_loop(0, dil, class_body, 0)

    rows_per_step = 256

    def merge_body(t, carry):
        t0 = pl.multiple_of(t * rows_per_step, rows_per_step)
        rows = pl.ds(t0, rows_per_step)
        for p in range(n_pairs):
            lse = [lse_scr[g, p, rows, :] for g in range(len(DA_DILATIONS))]
            mx = jnp.maximum(jnp.maximum(lse[0], lse[1]), lse[2])
            w = [jnp.exp2(v - mx) for v in lse]
            num = sum(w[g] * o_scr[g, p, rows, :] for g in range(len(DA_DILATIONS)))
            out_ref[rows, LANES * p:LANES * (p + 1)] = (num / (w[0] + w[1] + w[2])).astype(
                out_ref.dtype)
        return carry

    lax.fori_loop(0, T // rows_per_step, merge_body, 0)


def _dilated_attention(qkv):
    B, _, S, _ = qkv.shape
    T = DA_TOKENS
    n_blocks = S // T
    n_pairs = DA_WIDTH // LANES
    slopes = tuple(2.0 ** (-8.0 * (h + 1) / DA_HEADS) for h in range(DA_HEADS))

    def spec(slab, shift):
        def index(b, i):
            return (b, slab, jnp.clip(i + shift, 0, n_blocks - 1), 0)
        return pl.BlockSpec((None, n_pairs, T, LANES), index)

    win = T + 2 * DA_HALF_STEPS
    return pl.pallas_call(
        functools.partial(_dilated_kernel, n_blocks, slopes),
        grid=(B, n_blocks),
        in_specs=[spec(0, 0), spec(1, -1), spec(1, 0), spec(1, 1),
                  spec(2, -1), spec(2, 0), spec(2, 1)],
        out_specs=pl.BlockSpec((None, T, DA_WIDTH), lambda b, i: (b, i, 0)),
        out_shape=jax.ShapeDtypeStruct((B, S, DA_WIDTH), BF16),
        scratch_shapes=[
            pltpu.VMEM((n_pairs, T, LANES), BF16),
            pltpu.VMEM((n_pairs, win, LANES), BF16),
            pltpu.VMEM((n_pairs, win, LANES), BF16),
            pltpu.VMEM((len(DA_DILATIONS), n_pairs, T, LANES), F32),
            pltpu.VMEM((len(DA_DILATIONS), n_pairs, T, LANES), F32),
            pltpu.VMEM((DA_HEADS, 128, 128 + 2 * DA_HALF_STEPS), F32),
        ],
        compiler_params=_params("parallel", "parallel"),
        name="dilated_attention",
    )(qkv, qkv, qkv, qkv, qkv, qkv, qkv)


def _out_ffn_kernel(n_y, final_norm, *refs):
    x_ref = refs[0]
    y_refs = refs[1:1 + n_y]
    wo_ref, g_ref, wg_ref, wu_ref, wd_ref = refs[1 + n_y:6 + n_y]
    gf_ref = refs[6 + n_y] if final_norm else None
    out_ref = refs[-1]

    y = y_refs[0][...] if n_y == 1 else jnp.concatenate([r[...] for r in y_refs], axis=1)
    x = x_ref[...] + jnp.dot(y, wo_ref[...], preferred_element_type=F32)
    h = _rms(x, g_ref[...]).astype(BF16)
    gate = jnp.dot(h, wg_ref[...], preferred_element_type=F32)
    up = jnp.dot(h, wu_ref[...], preferred_element_type=F32)
    act = (jax.nn.silu(gate) * up).astype(BF16)
    x = x + jnp.dot(act, wd_ref[...], preferred_element_type=F32)
    if final_norm:
        x = _rms(x, gf_ref[...])
    out_ref[...] = x


def _out_ffn(x, ys, w_out, g, wg, wu, wd, g_final=None):
    B, S, D = x.shape
    rows = B * S
    tm = ROW_TILE
    x2 = x.reshape(rows, D)
    ys2 = [y.reshape(rows, y.shape[-1]) for y in ys]
    consts = [w_out, g, wg, wu, wd] + ([g_final] if g_final is not None else [])
    out = pl.pallas_call(
        functools.partial(_out_ffn_kernel, len(ys), g_final is not None),
        grid=(rows // tm,),
        in_specs=([pl.BlockSpec((tm, D), lambda i: (i, 0))]
                  + [pl.BlockSpec((tm, y.shape[-1]), lambda i: (i, 0)) for y in ys2]
                  + [pl.BlockSpec(t.shape, lambda i, nd=t.ndim: (0,) * nd,
                                  pipeline_mode=pl.Buffered(1)) for t in consts]),
        out_specs=pl.BlockSpec((tm, D), lambda i: (i, 0)),
        out_shape=jax.ShapeDtypeStruct((rows, D), F32),
        compiler_params=_params("parallel"),
        name="out_proj_ffn",
    )(x2, *ys2, *consts)
    return out.reshape(B, S, D)


def _mla_in_kernel(x_ref, g_ref, win_ref, qn_ref, kvn_ref, wq_ref, wk_ref, wvt_ref, cos_ref,
                   sin_ref, q_ref, k_ref, vt_ref):
    h = _rms(x_ref[...], g_ref[...]).astype(BF16)
    proj = jnp.dot(h, win_ref[...], preferred_element_type=F32)
    cq = _rms(proj[:, :MLA_Q_RANK], qn_ref[...]).astype(BF16)
    ckv = _rms(proj[:, MLA_Q_RANK:MLA_Q_RANK + MLA_KV_RANK], kvn_ref[...]).astype(BF16)
    base = MLA_Q_RANK + MLA_KV_RANK
    cos = cos_ref[...]
    sin = sin_ref[...]
    k_rope = proj[:, base:base + LANES] * cos + proj[:, base + LANES:base + 2 * LANES] * sin
    scale = math.log2(math.e) / math.sqrt(MLA_QK)
    width = MLA_HEADS * LANES
    for hd in range(MLA_HEADS):
        cols = slice(hd * LANES, (hd + 1) * LANES)
        rot_cols = slice(width + hd * LANES, width + (hd + 1) * LANES)
        qa = jnp.dot(cq, wq_ref[:, cols], preferred_element_type=F32)
        qb = jnp.dot(cq, wq_ref[:, rot_cols], preferred_element_type=F32)
        q_ref[:, cols] = ((qa * cos + qb * sin) * scale).astype(BF16)
        kn = jnp.dot(ckv, wk_ref[:, cols], preferred_element_type=F32)
        k_ref[:, cols] = (kn + k_rope).astype(BF16)
    vt = lax.dot_general(wvt_ref[...], ckv, (((1,), (1,)), ((), ())),
                         preferred_element_type=F32)
    ones = jnp.ones((MLA_ONES, vt.shape[1]), BF16)
    for hd in range(MLA_HEADS):
        vt_ref[hd * MLA_VROWS:hd * MLA_VROWS + MLA_V, :] = (
            vt[hd * MLA_V:(hd + 1) * MLA_V].astype(BF16))
        vt_ref[hd * MLA_VROWS + MLA_V:(hd + 1) * MLA_VROWS, :] = ones


def _mla_in_proj(x, g, win, qn, kvn, wq, wk, wvt, cos_tab, sin_tab):
    B, S, D = x.shape
    tm = ROW_TILE
    width = MLA_HEADS * LANES
    consts = [g, win, qn, kvn, wq, wk, wvt]
    return pl.pallas_call(
        _mla_in_kernel,
        grid=(B, S // tm),
        in_specs=([pl.BlockSpec((None, tm, D), lambda b, i: (b, i, 0))]
                  + [_const_spec(t.shape) for t in consts]
                  + [pl.BlockSpec((tm, LANES), lambda b, i: (i, 0)),
                     pl.BlockSpec((tm, LANES), lambda b, i: (i, 0))]),
        out_specs=[
            pl.BlockSpec((None, tm, width), lambda b, i: (b, i, 0)),
            pl.BlockSpec((None, tm, width), lambda b, i: (b, i, 0)),
            pl.BlockSpec((None, None, MLA_HEADS * MLA_VROWS, tm), lambda b, i: (b, i, 0, 0)),
        ],
        out_shape=[
            jax.ShapeDtypeStruct((B, S, width), BF16),
            jax.ShapeDtypeStruct((B, S, width), BF16),
            jax.ShapeDtypeStruct((B, S // tm, MLA_HEADS * MLA_VROWS, tm), BF16),
        ],
        compiler_params=_params("parallel", "parallel"),
        name="mla_in_proj",
    )(x, *consts, cos_tab, sin_tab)


def _mla_attn_kernel(q_ref, k_ref, vt_ref, o_ref, acc_scr):
    n_kb, _, bk = vt_ref.shape
    bq = q_ref.shape[0]
    qs = [q_ref[:, hh * LANES:(hh + 1) * LANES] for hh in range(2)]

    rows = vt_ref.shape[1] // 2

    def scores(j, hh):
        return lax.dot_general(k_ref[j * bk:(j + 1) * bk, hh * LANES:(hh + 1) * LANES], qs[hh],
                               (((1,), (1,)), ((), ())), preferred_element_type=F32)

    def update(j, hh, m, st):
        m_new = jnp.maximum(m, jnp.max(st, axis=0, keepdims=True))
        alpha = jnp.exp2(m - m_new)
        chunk = 16
        p = jnp.concatenate(
            [jnp.exp2(st[r:r + chunk] - m_new).astype(BF16) for r in range(0, bk, chunk)], axis=0)
        vb = vt_ref[j, hh * rows:(hh + 1) * rows, :]
        acc_scr[hh] = alpha * acc_scr[hh] + jnp.dot(vb, p, preferred_element_type=F32)
        return m_new

    acc_scr[...] = jnp.zeros_like(acc_scr)
    ms = [jnp.full((1, bq), NEG_BIG, F32) for _ in range(2)]
    ahead = 1
    queue = [[scores(j, hh) for hh in range(2)] for j in range(min(ahead, n_kb))]
    for j in range(n_kb):
        if j + ahead < n_kb:
            queue.append([scores(j + ahead, hh) for hh in range(2)])
        sts = queue.pop(0)
        ms = [update(j, hh, ms[hh], sts[hh]) for hh in range(2)]
    outs = [acc_scr[hh, 0:MLA_V] / acc_scr[hh, MLA_V:MLA_V + 1] for hh in range(2)]
    o_ref[...] = jnp.concatenate(outs, axis=0).T.astype(o_ref.dtype)


def _mla_attention(q, k, vt):
    B, S, _ = q.shape
    n_kb = vt.shape[1]
    bq = MLA_Q_TILE
    n_pairs = MLA_HEADS // 2
    return pl.pallas_call(
        _mla_attn_kernel,
        grid=(B, n_pairs, S // bq),
        in_specs=[
            pl.BlockSpec((None, bq, 2 * LANES), lambda b, p, i: (b, i, p)),
            pl.BlockSpec((None, S, 2 * LANES), lambda b, p, i: (b, 0, p)),
            pl.BlockSpec((None, n_kb, 2 * MLA_VROWS, MLA_K_TILE), lambda b, p, i: (b, 0, p, 0)),
        ],
        out_specs=pl.BlockSpec((None, bq, 2 * MLA_V), lambda b, p, i: (b, i, p)),
        out_shape=jax.ShapeDtypeStruct((B, S, MLA_HEADS * MLA_V), BF16),
        scratch_shapes=[
            pltpu.VMEM((2, MLA_VROWS, bq), F32),
        ],
        compiler_params=_params("parallel", "parallel", "arbitrary"),
        name="mla_attention",
    )(q, k, vt)


def _block_diag(w):
    nb, d, _ = w.shape
    eye = jnp.eye(nb, dtype=w.dtype)
    return (eye[:, None, :, None] * w[:, :, None, :]).reshape(nb * d, nb * d)


def _rot_cols(w):
    half = MLA_ROPE // 2
    return jnp.concatenate([-w[..., half:], w[..., :half]], axis=-1)


def _pad_lanes(w, offset):
    return jnp.pad(w, [(0, 0)] * (w.ndim - 1) + [(offset, LANES - offset - w.shape[-1])])


def _prepare(norm_mix, norm_ffn, norm_final, ab_w_in, ab_conv_w, ab_conv_b, rg_w_a, rg_b_a,
             rg_w_i, rg_b_i, rg_lam, ab_w_out, mla_w_in, mla_q_norm, mla_w_qb, mla_kv_norm,
             mla_w_kvb, mla_w_out, ffn_w_gate, ffn_w_up, ffn_w_down):
    row = lambda v: v.reshape(1, -1).astype(F32)
    P = {}
    P["norm_mix"] = [row(norm_mix[l]) for l in range(2)]
    P["norm_ffn"] = [row(norm_ffn[l]) for l in range(2)]
    P["norm_final"] = row(norm_final)
    P["ab_w_in"] = ab_w_in[0].astype(BF16)
    P["conv_w"] = ab_conv_w[0].astype(F32)
    P["conv_b"] = row(ab_conv_b[0])
    P["rg_w"] = [jnp.concatenate([_block_diag(rg_w_a[0, d]), _block_diag(rg_w_i[0, d])],
                                 axis=1).astype(BF16) for d in range(2)]
    P["rg_b"] = [jnp.concatenate([rg_b_a[0, d], rg_b_i[0, d]]).reshape(1, -1) for d in range(2)]
    P["rg_lam"] = [row(rg_lam[0, d]) for d in range(2)]
    P["ab_w_out"] = ab_w_out[0].astype(BF16)

    w_in = mla_w_in[0]
    base = MLA_Q_RANK + MLA_KV_RANK
    w_kr = w_in[:, base:]
    P["mla_w_in"] = jnp.concatenate(
        [w_in[:, :base], _pad_lanes(w_kr, MLA_NOPE), _pad_lanes(_rot_cols(w_kr), MLA_NOPE)],
        axis=1).astype(BF16)
    P["mla_q_norm"] = row(mla_q_norm[0])
    P["mla_kv_norm"] = row(mla_kv_norm[0])
    wq = mla_w_qb[0].reshape(MLA_Q_RANK, MLA_HEADS, MLA_QK)
    wq_plain = _pad_lanes(wq, 0)
    wq_rot = _pad_lanes(_rot_cols(wq[..., MLA_NOPE:]), MLA_NOPE)
    P["mla_wq"] = jnp.concatenate(
        [wq_plain.reshape(MLA_Q_RANK, -1), wq_rot.reshape(MLA_Q_RANK, -1)], axis=1).astype(BF16)
    wkv = mla_w_kvb[0].reshape(MLA_KV_RANK, MLA_HEADS, MLA_NOPE + MLA_V)
    P["mla_wk"] = _pad_lanes(wkv[..., :MLA_NOPE], 0).reshape(MLA_KV_RANK, -1).astype(BF16)
    P["mla_wvt"] = wkv[..., MLA_NOPE:].reshape(MLA_KV_RANK, -1).T.astype(BF16)
    P["mla_w_out"] = mla_w_out[0].astype(BF16)
    P["ffn"] = [(ffn_w_gate[l].astype(BF16), ffn_w_up[l].astype(BF16),
                 ffn_w_down[l].astype(BF16)) for l in range(2)]
    return P


def _rope_slabs(S):
    inv_freq = 1.0 / (ROPE_THETA ** (jnp.arange(0, MLA_ROPE, 2, dtype=F32) / MLA_ROPE))
    ang = jnp.arange(S, dtype=F32)[:, None] * inv_freq[None, :]
    cos, sin = jnp.cos(ang), jnp.sin(ang)
    pad = LANES - MLA_QK
    cos_tab = jnp.concatenate([jnp.ones((S, MLA_NOPE), F32), cos, cos, jnp.zeros((S, pad), F32)], 1)
    sin_tab = jnp.concatenate([jnp.zeros((S, MLA_NOPE), F32), sin, sin, jnp.zeros((S, pad), F32)], 1)
    return cos_tab, sin_tab


def _trunk(x, P):
    S = x.shape[1]
    xg, qkv = _ab_in_proj(x, P["norm_mix"][0], P["ab_w_in"])
    hf = _rglru_pass(xg, None, P["conv_w"], P["conv_b"], P["rg_w"][0], P["rg_b"][0],
                     P["rg_lam"][0], reverse=False)
    y_rnn = _rglru_pass(xg, hf, P["conv_w"], P["conv_b"], P["rg_w"][1], P["rg_b"][1],
                        P["rg_lam"][1], reverse=True)
    o = _dilated_attention(qkv)
    x = _out_ffn(x, [y_rnn, o], P["ab_w_out"], P["norm_ffn"][0], *P["ffn"][0])
    cos_tab, sin_tab = _rope_slabs(S)
    q, k, vt = _mla_in_proj(x, P["norm_mix"][1], P["mla_w_in"], P["mla_q_norm"], P["mla_kv_norm"],
                            P["mla_wq"], P["mla_wk"], P["mla_wvt"], cos_tab, sin_tab)
    o = _mla_attention(q, k, vt)
    return _out_ffn(x, [o], P["mla_w_out"], P["norm_ffn"][1], *P["ffn"][1],
                    g_final=P["norm_final"])


def kernel(x_prompt, x_sample, norm_mix, norm_ffn, norm_final, ab_w_in, ab_conv_w, ab_conv_b, rg_w_a, rg_b_a, rg_w_i, rg_b_i, rg_lam, ab_w_out, mla_w_in, mla_q_norm, mla_w_qb, mla_kv_norm, mla_w_kvb, mla_w_out, ffn_w_gate, ffn_w_up, ffn_w_down):
    P = _prepare(norm_mix, norm_ffn, norm_final, ab_w_in, ab_conv_w, ab_conv_b, rg_w_a, rg_b_a,
                 rg_w_i, rg_b_i, rg_lam, ab_w_out, mla_w_in, mla_q_norm, mla_w_qb, mla_kv_norm,
                 mla_w_kvb, mla_w_out, ffn_w_gate, ffn_w_up, ffn_w_down)
    return (_trunk(x_prompt, P), _trunk(x_sample, P))
```

```python
import functools
import math

import jax
import jax.numpy as jnp
from jax import lax
from jax.experimental import pallas as pl
from jax.experimental.pallas import tpu as pltpu

F32 = jnp.float32
BF16 = jnp.bfloat16

D_MODEL = 1024
EPS = 1e-6
RG_WIDTH = 512
RG_BLOCKS = 8
RG_C = 8.0
DA_HEADS = 8
DA_HEAD_DIM = 64
DA_WIDTH = 512
DA_DILATIONS = (1, 4, 16)
DA_HALF_STEPS = 64
DA_TOKENS = DA_HALF_STEPS * max(DA_DILATIONS)
DA_Q_SLOT = 128
DA_K_SLOT = 256
MLA_HEADS = 16
MLA_Q_RANK = 384
MLA_KV_RANK = 256
MLA_NOPE = 64
MLA_ROPE = 32
MLA_V = 64
MLA_QK = MLA_NOPE + MLA_ROPE
MLA_ONES = 16
MLA_VROWS = MLA_V + MLA_ONES
MLA_MAX_LAG_EXCESS = 100.0
ROPE_THETA = 10000.0
FFN_HIDDEN = 2816
NEG_BIG = -1e30
LANES = 128

VMEM_LIMIT = 56 * 1024 * 1024

ROW_TILE = 512
SCAN_TILE = 512
SCAN_PAD = 8
MLA_Q_TILE = 1024
MLA_K_TILE = ROW_TILE


def _params(*sem):
    return pltpu.CompilerParams(dimension_semantics=sem, vmem_limit_bytes=VMEM_LIMIT)


def _const_spec(shape):
    nd = len(shape)
    return pl.BlockSpec(shape, lambda *_: (0,) * nd)


def _rms(x, g):
    return x * lax.rsqrt(jnp.mean(x * x, axis=-1, keepdims=True) + EPS) * g


def _ab_in_kernel(x_ref, g_ref, w_ref, xg_ref, qkv_ref):
    h = _rms(x_ref[...], g_ref[...]).astype(BF16)
    xg_ref[...] = jnp.dot(h, w_ref[:, :2 * RG_WIDTH], preferred_element_type=F32)
    n_slabs = 3 * DA_WIDTH // LANES
    qkv = jnp.dot(h, w_ref[:, 2 * RG_WIDTH:], preferred_element_type=F32)
    for j in range(n_slabs):
        qkv_ref[j] = qkv[:, LANES * j:LANES * (j + 1)]


def _ab_in_proj(x, g, w):
    B, S, D = x.shape
    n_slabs = 3 * DA_WIDTH // LANES
    tm = ROW_TILE
    return pl.pallas_call(
        _ab_in_kernel,
        grid=(B, S // tm),
        in_specs=[
            pl.BlockSpec((None, tm, D), lambda b, i: (b, i, 0)),
            _const_spec(g.shape),
            _const_spec(w.shape),
        ],
        out_specs=[
            pl.BlockSpec((None, tm, 2 * RG_WIDTH), lambda b, i: (b, i, 0)),
            pl.BlockSpec((None, n_slabs, tm, LANES), lambda b, i: (b, 0, i, 0)),
        ],
        out_shape=[
            jax.ShapeDtypeStruct((B, S, 2 * RG_WIDTH), F32),
            jax.ShapeDtypeStruct((B, n_slabs, S, LANES), F32),
        ],
        compiler_params=_params("parallel", "parallel"),
        name="ab_in_proj",
    )(x, g, w)


def _shift_rows(x, d, edge, toward_end):
    t = SCAN_PAD
    n = x.shape[0] // t
    sub = lax.broadcasted_iota(jnp.int32, (t, x.shape[1]), 0)
    tiles = [x[t * i:t * (i + 1)] for i in range(n)]
    if toward_end:
        rot = [pltpu.roll(b, d, 0) for b in [edge] + tiles]
        out = [jnp.where(sub < d, rot[i], rot[i + 1]) for i in range(n)]
    else:
        rot = [pltpu.roll(b, t - d, 0) for b in tiles + [edge]]
        out = [jnp.where(sub >= t - d, rot[i + 1], rot[i]) for i in range(n)]
    return jnp.concatenate(out, axis=0)


def _rglru_kernel(reverse, n_chunks, *refs):
    if reverse:
        (x_ref, xp_ref, xn_ref, gate_ref, hf_ref, cw_ref, cb_ref, wg_ref, bg_ref, lam_ref,
         out_ref, carry_ref) = refs
    else:
        (x_ref, xp_ref, xn_ref, cw_ref, cb_ref, wg_ref, bg_ref, lam_ref,
         out_ref, carry_ref) = refs
    i = pl.program_id(1)
    c = (n_chunks - 1 - i) if reverse else i
    tc, width = x_ref.shape

    @pl.when(i == 0)
    def _():
        carry_ref[...] = jnp.zeros_like(carry_ref)

    x = x_ref[...]
    prev = jnp.where(c > 0, xp_ref[...], 0.0)
    nxt = jnp.where(c < n_chunks - 1, xn_ref[...], 0.0)
    cw = cw_ref[...]
    xc = (cw[0:1] * _shift_rows(x, 2, prev, True) + cw[1:2] * _shift_rows(x, 1, prev, True)
          + cw[2:3] * x + cw[3:4] * _shift_rows(x, 1, nxt, False) + cb_ref[...])

    z = jnp.dot(xc.astype(BF16), wg_ref[...], preferred_element_type=F32) + bg_ref[...]
    r = jax.nn.sigmoid(z[:, :width])
    gi = jax.nn.sigmoid(z[:, width:])
    log_a = (-RG_C) * r * jax.nn.softplus(-lam_ref[...])
    a = jnp.exp(log_a)
    u = jnp.sqrt(-jnp.tanh(log_a) * (a * a + 1.0)) * (gi * xc)

    one_edge = jnp.ones((SCAN_PAD, width), F32)
    zero_edge = jnp.zeros((SCAN_PAD, width), F32)
    d = 1
    while d < tc:
        if d < SCAN_PAD:
            a_sh = _shift_rows(a, d, one_edge, not reverse)
            u = a * _shift_rows(u, d, zero_edge, not reverse) + u
            a = a * a_sh
        elif reverse:
            u = jnp.concatenate([a[:tc - d] * u[d:] + u[:tc - d], u[tc - d:]], axis=0)
            a = jnp.concatenate([a[:tc - d] * a[d:], a[tc - d:]], axis=0)
        else:
            u = jnp.concatenate([u[:d], a[d:] * u[:tc - d] + u[d:]], axis=0)
            a = jnp.concatenate([a[:d], a[d:] * a[:tc - d]], axis=0)
        d *= 2
    h = u + a * carry_ref[...]
    carry_ref[...] = h[0:1] if reverse else h[tc - 1:tc]

    if reverse:
        out_ref[...] = (jax.nn.gelu(gate_ref[...]) * (hf_ref[...] + h)).astype(out_ref.dtype)
    else:
        out_ref[...] = h


def _rglru_pass(xg, hf, conv_w, conv_b, wg, bg, lam, reverse):
    B, S, _ = xg.shape
    tc = SCAN_TILE
    n_chunks = S // tc
    sub = SCAN_PAD

    def chunk(i):
        return (n_chunks - 1 - i) if reverse else i

    x_spec = pl.BlockSpec((None, tc, RG_WIDTH), lambda b, i: (b, chunk(i), 0))
    prev_spec = pl.BlockSpec(
        (None, sub, RG_WIDTH), lambda b, i: (b, jnp.maximum(chunk(i) * (tc // sub) - 1, 0), 0))
    next_spec = pl.BlockSpec(
        (None, sub, RG_WIDTH),
        lambda b, i: (b, jnp.minimum((chunk(i) + 1) * (tc // sub), S // sub - 1), 0))
    consts = [conv_w, conv_b, wg, bg, lam]
    const_specs = [_const_spec(t.shape) for t in consts]
    if reverse:
        gate_spec = pl.BlockSpec((None, tc, RG_WIDTH), lambda b, i: (b, chunk(i), 1))
        hf_spec = pl.BlockSpec((None, tc, RG_WIDTH), lambda b, i: (b, chunk(i), 0))
        inputs = [xg, xg, xg, xg, hf] + consts
        in_specs = [x_spec, prev_spec, next_spec, gate_spec, hf_spec] + const_specs
        out_dtype = BF16
    else:
        inputs = [xg, xg, xg] + consts
        in_specs = [x_spec, prev_spec, next_spec] + const_specs
        out_dtype = F32
    return pl.pallas_call(
        functools.partial(_rglru_kernel, reverse, n_chunks),
        grid=(B, n_chunks),
        in_specs=in_specs,
        out_specs=pl.BlockSpec((None, tc, RG_WIDTH), lambda b, i: (b, chunk(i), 0)),
        out_shape=jax.ShapeDtypeStruct((B, S, RG_WIDTH), out_dtype),
        scratch_shapes=[pltpu.VMEM((1, RG_WIDTH), F32)],
        compiler_params=_params("parallel", "arbitrary"),
        name="rglru_bwd" if reverse else "rglru_fwd",
    )(*inputs)


def _dilated_kernel(n_blocks, slopes, q_ref, kp_ref, kc_ref, kn_ref, vp_ref, vc_ref, vn_ref,
                    out_ref, qs_ref, ks_ref, vs_ref, o_scr, lse_scr, bias_scr):
    i = pl.program_id(1)
    T = DA_TOKENS
    W = DA_HALF_STEPS
    n_pairs = DA_WIDTH // LANES
    log2e = math.log2(math.e)
    scale = log2e / math.sqrt(DA_HEAD_DIM)
    is_first = i == 0
    is_last = i == n_blocks - 1

    for g, dil in enumerate(DA_DILATIONS):
        n = T // dil
        bq = min(n, 128)
        bk = bq + 2 * W
        rowq = lax.broadcasted_iota(jnp.int32, (bq, bk), 0)
        colk = lax.broadcasted_iota(jnp.int32, (bq, bk), 1)
        rel = colk - W - rowq
        band = jnp.abs(rel) <= W
        neg_dist = -(jnp.abs(rel) * dil).astype(F32)
        lane = lax.broadcasted_iota(jnp.int32, (bq, LANES), 1)
        lo_lane = lane < DA_HEAD_DIM
        for head in range(DA_HEADS):
            bias_scr[head, 0:bq, 0:bk] = jnp.where(band, (slopes[head] * log2e) * neg_dist,
                                                   NEG_BIG)

        def stage(r, slot, dil=dil, n=n):
            qo, ko = slot * DA_Q_SLOT, slot * DA_K_SLOT
            for p in range(n_pairs):
                qs_ref[p, qo:qo + n] = (q_ref[p, pl.ds(r, n, stride=dil), :] * scale).astype(BF16)
                for src_p, src_c, src_n, dst in ((kp_ref, kc_ref, kn_ref, ks_ref),
                                                 (vp_ref, vc_ref, vn_ref, vs_ref)):
                    dst[p, ko:ko + W] = src_p[p, pl.ds(r + dil * (n - W), W, stride=dil),
                                              :].astype(BF16)
                    dst[p, ko + W:ko + W + n] = src_c[p, pl.ds(r, n, stride=dil), :].astype(BF16)
                    dst[p, ko + W + n:ko + 2 * W + n] = src_n[p, pl.ds(r, W, stride=dil),
                                                             :].astype(BF16)

        def attend(items, dil=dil, n=n, bq=bq, bk=bk, g=g, colk=colk, lo_lane=lo_lane):
            def start(q0, slot, pitch):
                return q0 if slot == 0 else slot * pitch + q0

            scores = []
            for r, q0, slot in items:
                for p in range(n_pairs):
                    qp = qs_ref[p, pl.ds(start(q0, slot, DA_Q_SLOT), bq), :]
                    kp = ks_ref[p, pl.ds(start(q0, slot, DA_K_SLOT), bk), :]
                    for hh in range(2):
                        qh = jnp.where(lo_lane if hh == 0 else jnp.logical_not(lo_lane), qp,
                                       jnp.zeros_like(qp))
                        scores.append(lax.dot_general(qh, kp, (((1,), (1,)), ((), ())),
                                                      preferred_element_type=F32))
            probs, lse2, inv_l = [], [], []
            for idx, s in enumerate(scores):
                r, q0, slot = items[idx // DA_HEADS]
                pos = q0 + colk
                in_seq = jnp.logical_not(
                    jnp.logical_or(jnp.logical_and(is_first, pos < W),
                                   jnp.logical_and(is_last, pos >= n + W)))
                s = jnp.where(in_seq, s + bias_scr[idx % DA_HEADS, 0:bq, 0:bk], NEG_BIG)
                m = jnp.max(s, axis=-1, keepdims=True)
                e = jnp.exp2(s - m)
                l = jnp.sum(e, axis=-1, keepdims=True)
                probs.append(e.astype(BF16))
                inv_l.append(1.0 / l)
                lse2.append(m + jnp.log2(l))
            for it, (r, q0, slot) in enumerate(items):
                for p in range(n_pairs):
                    vp = vs_ref[p, pl.ds(start(q0, slot, DA_K_SLOT), bk), :]
                    h0 = it * DA_HEADS + 2 * p
                    o_h = [jnp.dot(probs[h0 + hh], vp, preferred_element_type=F32)
                           * inv_l[h0 + hh] for hh in range(2)]
                    rows = pl.ds(r + dil * q0, bq, stride=dil)
                    o_scr[g, p, rows, :] = jnp.where(lo_lane, o_h[0], o_h[1])
                    lse_scr[g, p, rows, :] = jnp.where(lo_lane, lse2[h0], lse2[h0 + 1])

        if n == bq:
            def class_pair(t, carry, stage=stage, attend=attend):
                stage(2 * t, 0)
                stage(2 * t + 1, 1)
                attend([(2 * t, 0, 0), (2 * t + 1, 0, 1)])
                return carry

            lax.fori_loop(0, dil // 2, class_pair, 0)
        else:
            def class_body(r, carry, stage=stage, attend=attend, bq=bq, n=n):
                stage(r, 0)

                def qb_pair(t, carry2):
                    q0 = pl.multiple_of(t * (2 * bq), 2 * bq)
                    attend([(r, q0, 0), (r, pl.multiple_of(q0 + bq, bq), 0)])
                    return carry2

                lax.fori_loop(0, n // (2 * bq), qb_pair, 0)
                return carry

            lax.fori_loop(0, dil, class_body, 0)

    rows_per_step = 256

    def merge_body(t, carry):
        t0 = pl.multiple_of(t * rows_per_step, rows_per_step)
        rows = pl.ds(t0, rows_per_step)
        for p in range(n_pairs):
            lse = [lse_scr[g, p, rows, :] for g in range(len(DA_DILATIONS))]
            mx = jnp.maximum(jnp.maximum(lse[0], lse[1]), lse[2])
            w = [jnp.exp2(v - mx) for v in lse]
            num = sum(w[g] * o_scr[g, p, rows, :] for g in range(len(DA_DILATIONS)))
            out_ref[rows, LANES * p:LANES * (p + 1)] = (num / (w[0] + w[1] + w[2])).astype(
                out_ref.dtype)
        return carry

    lax.fori_loop(0, T // rows_per_step, merge_body, 0)


def _dilated_attention(qkv):
    B, _, S, _ = qkv.shape
    T = DA_TOKENS
    n_blocks = S // T
    n_pairs = DA_WIDTH // LANES
    slopes = tuple(2.0 ** (-8.0 * (h + 1) / DA_HEADS) for h in range(DA_HEADS))

    def spec(slab, shift):
        def index(b, i):
            return (b, slab, jnp.clip(i + shift, 0, n_blocks - 1), 0)
        return pl.BlockSpec((None, n_pairs, T, LANES), index)

    win = T + 2 * DA_HALF_STEPS
    return pl.pallas_call(
        functools.partial(_dilated_kernel, n_blocks, slopes),
        grid=(B, n_blocks),
        in_specs=[spec(0, 0), spec(1, -1), spec(1, 0), spec(1, 1),
                  spec(2, -1), spec(2, 0), spec(2, 1)],
        out_specs=pl.BlockSpec((None, T, DA_WIDTH), lambda b, i: (b, i, 0)),
        out_shape=jax.ShapeDtypeStruct((B, S, DA_WIDTH), BF16),
        scratch_shapes=[
            pltpu.VMEM((n_pairs, T, LANES), BF16),
            pltpu.VMEM((n_pairs, win, LANES), BF16),
            pltpu.VMEM((n_pairs, win, LANES), BF16),
            pltpu.VMEM((len(DA_DILATIONS), n_pairs, T, LANES), F32),
            pltpu.VMEM((len(DA_DILATIONS), n_pairs, T, LANES), F32),
            pltpu.VMEM((DA_HEADS, 128, 128 + 2 * DA_HALF_STEPS), F32),
        ],
        compiler_params=_params("parallel", "parallel"),
        name="dilated_attention",
    )(qkv, qkv, qkv, qkv, qkv, qkv, qkv)


def _out_ffn_kernel(n_y, final_norm, *refs):
    x_ref = refs[0]
    y_refs = refs[1:1 + n_y]
    wo_ref, g_ref, wg_ref, wu_ref, wd_ref = refs[1 + n_y:6 + n_y]
    gf_ref = refs[6 + n_y] if final_norm else None
    out_ref = refs[-1]

    y = y_refs[0][...] if n_y == 1 else jnp.concatenate([r[...] for r in y_refs], axis=1)
    x = x_ref[...] + jnp.dot(y, wo_ref[...], preferred_element_type=F32)
    h = _rms(x, g_ref[...]).astype(BF16)
    gate = jnp.dot(h, wg_ref[...], preferred_element_type=F32)
    up = jnp.dot(h, wu_ref[...], preferred_element_type=F32)
    act = (jax.nn.silu(gate) * up).astype(BF16)
    x = x + jnp.dot(act, wd_ref[...], preferred_element_type=F32)
    if final_norm:
        x = _rms(x, gf_ref[...])
    out_ref[...] = x


def _out_ffn(x, ys, w_out, g, wg, wu, wd, g_final=None):
    B, S, D = x.shape
    rows = B * S
    tm = ROW_TILE
    x2 = x.reshape(rows, D)
    ys2 = [y.reshape(rows, y.shape[-1]) for y in ys]
    consts = [w_out, g, wg, wu, wd] + ([g_final] if g_final is not None else [])
    out = pl.pallas_call(
        functools.partial(_out_ffn_kernel, len(ys), g_final is not None),
        grid=(rows // tm,),
        in_specs=([pl.BlockSpec((tm, D), lambda i: (i, 0))]
                  + [pl.BlockSpec((tm, y.shape[-1]), lambda i: (i, 0)) for y in ys2]
                  + [pl.BlockSpec(t.shape, lambda i, nd=t.ndim: (0,) * nd,
                                  pipeline_mode=pl.Buffered(1)) for t in consts]),
        out_specs=pl.BlockSpec((tm, D), lambda i: (i, 0)),
        out_shape=jax.ShapeDtypeStruct((rows, D), F32),
        compiler_params=_params("parallel"),
        name="out_proj_ffn",
    )(x2, *ys2, *consts)
    return out.reshape(B, S, D)


def _mla_in_kernel(x_ref, g_ref, win_ref, qn_ref, kvn_ref, wq_ref, wk_ref, wvt_ref, cos_ref,
                   sin_ref, q_ref, k_ref, vt_ref):
    h = _rms(x_ref[...], g_ref[...]).astype(BF16)
    proj = jnp.dot(h, win_ref[...], preferred_element_type=F32)
    cq = _rms(proj[:, :MLA_Q_RANK], qn_ref[...]).astype(BF16)
    ckv = _rms(proj[:, MLA_Q_RANK:MLA_Q_RANK + MLA_KV_RANK], kvn_ref[...]).astype(BF16)
    base = MLA_Q_RANK + MLA_KV_RANK
    cos = cos_ref[...]
    sin = sin_ref[...]
    k_rope = proj[:, base:base + LANES] * cos + proj[:, base + LANES:base + 2 * LANES] * sin
    scale = math.log2(math.e) / math.sqrt(MLA_QK)
    width = MLA_HEADS * LANES
    for hd in range(MLA_HEADS):
        cols = slice(hd * LANES, (hd + 1) * LANES)
        rot_cols = slice(width + hd * LANES, width + (hd + 1) * LANES)
        qa = jnp.dot(cq, wq_ref[:, cols], preferred_element_type=F32)
        qb = jnp.dot(cq, wq_ref[:, rot_cols], preferred_element_type=F32)
        q_ref[:, cols] = ((qa * cos + qb * sin) * scale).astype(BF16)
        kn = jnp.dot(ckv, wk_ref[:, cols], preferred_element_type=F32)
        k_ref[:, cols] = (kn + k_rope).astype(BF16)
    vt = lax.dot_general(wvt_ref[...], ckv, (((1,), (1,)), ((), ())),
                         preferred_element_type=F32)
    ones = jnp.ones((MLA_ONES, vt.shape[1]), BF16)
    for hd in range(MLA_HEADS):
        vt_ref[hd * MLA_VROWS:hd * MLA_VROWS + MLA_V, :] = (
            vt[hd * MLA_V:(hd + 1) * MLA_V].astype(BF16))
        vt_ref[hd * MLA_VROWS + MLA_V:(hd + 1) * MLA_VROWS, :] = ones


def _mla_in_proj(x, g, win, qn, kvn, wq, wk, wvt, cos_tab, sin_tab):
    B, S, D = x.shape
    tm = ROW_TILE
    width = MLA_HEADS * LANES
    consts = [g, win, qn, kvn, wq, wk, wvt]
    return pl.pallas_call(
        _mla_in_kernel,
        grid=(B, S // tm),
        in_specs=([pl.BlockSpec((None, tm, D), lambda b, i: (b, i, 0))]
                  + [_const_spec(t.shape) for t in consts]
                  + [pl.BlockSpec((tm, LANES), lambda b, i: (i, 0)),
                     pl.BlockSpec((tm, LANES), lambda b, i: (i, 0))]),
        out_specs=[
            pl.BlockSpec((None, tm, width), lambda b, i: (b, i, 0)),
            pl.BlockSpec((None, tm, width), lambda b, i: (b, i, 0)),
            pl.BlockSpec((None, None, MLA_HEADS * MLA_VROWS, tm), lambda b, i: (b, i, 0, 0)),
        ],
        out_shape=[
            jax.ShapeDtypeStruct((B, S, width), BF16),
            jax.ShapeDtypeStruct((B, S, width), BF16),
            jax.ShapeDtypeStruct((B, S // tm, MLA_HEADS * MLA_VROWS, tm), BF16),
        ],
        compiler_params=_params("parallel", "parallel"),
        name="mla_in_proj",
    )(x, *consts, cos_tab, sin_tab)


def _mla_attn_kernel(q_ref, k_ref, vt_ref, o_ref, acc_scr):
    n_kb, _, bk = vt_ref.shape
    bq = q_ref.shape[0]
    qs = [q_ref[:, hh * LANES:(hh + 1) * LANES] for hh in range(2)]
    rows = vt_ref.shape[1] // 2
    nt = (((1,), (1,)), ((), ()))

    def scores(j, hh):
        return lax.dot_general(k_ref[j * bk:(j + 1) * bk, hh * LANES:(hh + 1) * LANES], qs[hh],
                               nt, preferred_element_type=F32)

    def write_output():
        outs = [acc_scr[hh, 0:MLA_V] / acc_scr[hh, MLA_V:MLA_V + 1] for hh in range(2)]
        o_ref[...] = jnp.concatenate(outs, axis=0).T.astype(o_ref.dtype)

    ref_max = [None, None]
    run_max = [None, None]
    excess = jnp.zeros((1, bq), F32)
    sts = [scores(0, hh) for hh in range(2)]
    for j in range(n_kb):
        for hh in range(2):
            blk_max = jnp.max(sts[hh], axis=0, keepdims=True)
            if j == 0:
                new_ref, alpha = blk_max, None
                run_max[hh] = blk_max
            else:
                new_ref = run_max[hh]
                alpha = jnp.exp2(ref_max[hh] - new_ref)
                excess = jnp.maximum(excess, blk_max - new_ref)
                run_max[hh] = jnp.maximum(run_max[hh], blk_max)
            ref_max[hh] = new_ref
            p = jnp.exp2(sts[hh] - new_ref).astype(BF16)
            if j + 1 < n_kb:
                sts[hh] = scores(j + 1, hh)
            pv = jnp.dot(vt_ref[j, hh * rows:(hh + 1) * rows, :], p, preferred_element_type=F32)
            acc_scr[hh] = pv if alpha is None else alpha * acc_scr[hh] + pv
    write_output()

    @pl.when(jnp.max(excess) > MLA_MAX_LAG_EXCESS)
    def _():
        acc_scr[...] = jnp.zeros_like(acc_scr)

        def body(j, ms):
            k0 = pl.multiple_of(j * bk, bk)
            new = []
            for hh in range(2):
                st = lax.dot_general(k_ref[pl.ds(k0, bk), hh * LANES:(hh + 1) * LANES], qs[hh],
                                     nt, preferred_element_type=F32)
                m_new = jnp.maximum(ms[hh], jnp.max(st, axis=0, keepdims=True))
                p = jnp.exp2(st - m_new).astype(BF16)
                pv = jnp.dot(vt_ref[j, hh * rows:(hh + 1) * rows, :], p,
                             preferred_element_type=F32)
                acc_scr[hh] = jnp.exp2(ms[hh] - m_new) * acc_scr[hh] + pv
                new.append(m_new)
            return tuple(new)

        lax.fori_loop(0, n_kb, body, tuple(jnp.full((1, bq), NEG_BIG, F32) for _ in range(2)))
        write_output()


def _mla_attention(q, k, vt):
    B, S, _ = q.shape
    n_kb = vt.shape[1]
    bq = MLA_Q_TILE
    n_pairs = MLA_HEADS // 2
    return pl.pallas_call(
        _mla_attn_kernel,
        grid=(B, n_pairs, S // bq),
        in_specs=[
            pl.BlockSpec((None, bq, 2 * LANES), lambda b, p, i: (b, i, p)),
            pl.BlockSpec((None, S, 2 * LANES), lambda b, p, i: (b, 0, p)),
            pl.BlockSpec((None, n_kb, 2 * MLA_VROWS, MLA_K_TILE), lambda b, p, i: (b, 0, p, 0)),
        ],
        out_specs=pl.BlockSpec((None, bq, 2 * MLA_V), lambda b, p, i: (b, i, p)),
        out_shape=jax.ShapeDtypeStruct((B, S, MLA_HEADS * MLA_V), BF16),
        scratch_shapes=[
            pltpu.VMEM((2, MLA_VROWS, bq), F32),
        ],
        compiler_params=_params("parallel", "parallel", "arbitrary"),
        name="mla_attention",
    )(q, k, vt)


def _block_diag(w):
    nb, d, _ = w.shape
    eye = jnp.eye(nb, dtype=w.dtype)
    return (eye[:, None, :, None] * w[:, :, None, :]).reshape(nb * d, nb * d)


def _rot_cols(w):
    half = MLA_ROPE // 2
    return jnp.concatenate([-w[..., half:], w[..., :half]], axis=-1)


def _pad_lanes(w, offset):
    return jnp.pad(w, [(0, 0)] * (w.ndim - 1) + [(offset, LANES - offset - w.shape[-1])])


def _prepare(norm_mix, norm_ffn, norm_final, ab_w_in, ab_conv_w, ab_conv_b, rg_w_a, rg_b_a,
             rg_w_i, rg_b_i, rg_lam, ab_w_out, mla_w_in, mla_q_norm, mla_w_qb, mla_kv_norm,
             mla_w_kvb, mla_w_out, ffn_w_gate, ffn_w_up, ffn_w_down):
    row = lambda v: v.reshape(1, -1).astype(F32)
    P = {}
    P["norm_mix"] = [row(norm_mix[l]) for l in range(2)]
    P["norm_ffn"] = [row(norm_ffn[l]) for l in range(2)]
    P["norm_final"] = row(norm_final)
    P["ab_w_in"] = ab_w_in[0].astype(BF16)
    P["conv_w"] = ab_conv_w[0].astype(F32)
    P["conv_b"] = row(ab_conv_b[0])
    P["rg_w"] = [jnp.concatenate([_block_diag(rg_w_a[0, d]), _block_diag(rg_w_i[0, d])],
                                 axis=1).astype(BF16) for d in range(2)]
    P["rg_b"] = [jnp.concatenate([rg_b_a[0, d], rg_b_i[0, d]]).reshape(1, -1) for d in range(2)]
    P["rg_lam"] = [row(rg_lam[0, d]) for d in range(2)]
    P["ab_w_out"] = ab_w_out[0].astype(BF16)

    w_in = mla_w_in[0]
    base = MLA_Q_RANK + MLA_KV_RANK
    w_kr = w_in[:, base:]
    P["mla_w_in"] = jnp.concatenate(
        [w_in[:, :base], _pad_lanes(w_kr, MLA_NOPE), _pad_lanes(_rot_cols(w_kr), MLA_NOPE)],
        axis=1).astype(BF16)
    P["mla_q_norm"] = row(mla_q_norm[0])
    P["mla_kv_norm"] = row(mla_kv_norm[0])
    wq = mla_w_qb[0].reshape(MLA_Q_RANK, MLA_HEADS, MLA_QK)
    wq_plain = _pad_lanes(wq, 0)
    wq_rot = _pad_lanes(_rot_cols(wq[..., MLA_NOPE:]), MLA_NOPE)
    P["mla_wq"] = jnp.concatenate(
        [wq_plain.reshape(MLA_Q_RANK, -1), wq_rot.reshape(MLA_Q_RANK, -1)], axis=1).astype(BF16)
    wkv = mla_w_kvb[0].reshape(MLA_KV_RANK, MLA_HEADS, MLA_NOPE + MLA_V)
    P["mla_wk"] = _pad_lanes(wkv[..., :MLA_NOPE], 0).reshape(MLA_KV_RANK, -1).astype(BF16)
    P["mla_wvt"] = wkv[..., MLA_NOPE:].reshape(MLA_KV_RANK, -1).T.astype(BF16)
    P["mla_w_out"] = mla_w_out[0].astype(BF16)
    P["ffn"] = [(ffn_w_gate[l].astype(BF16), ffn_w_up[l].astype(BF16),
                 ffn_w_down[l].astype(BF16)) for l in range(2)]
    return P


def _rope_slabs(S):
    inv_freq = 1.0 / (ROPE_THETA ** (jnp.arange(0, MLA_ROPE, 2, dtype=F32) / MLA_ROPE))
    ang = jnp.arange(S, dtype=F32)[:, None] * inv_freq[None, :]
    cos, sin = jnp.cos(ang), jnp.sin(ang)
    pad = LANES - MLA_QK
    cos_tab = jnp.concatenate([jnp.ones((S, MLA_NOPE), F32), cos, cos, jnp.zeros((S, pad), F32)], 1)
    sin_tab = jnp.concatenate([jnp.zeros((S, MLA_NOPE), F32), sin, sin, jnp.zeros((S, pad), F32)], 1)
    return cos_tab, sin_tab


def _trunk(x, P):
    S = x.shape[1]
    xg, qkv = _ab_in_proj(x, P["norm_mix"][0], P["ab_w_in"])
    hf = _rglru_pass(xg, None, P["conv_w"], P["conv_b"], P["rg_w"][0], P["rg_b"][0],
                     P["rg_lam"][0], reverse=False)
    y_rnn = _rglru_pass(xg, hf, P["conv_w"], P["conv_b"], P["rg_w"][1], P["rg_b"][1],
                        P["rg_lam"][1], reverse=True)
    o = _dilated_attention(qkv)
    x = _out_ffn(x, [y_rnn, o], P["ab_w_out"], P["norm_ffn"][0], *P["ffn"][0])
    cos_tab, sin_tab = _rope_slabs(S)
    q, k, vt = _mla_in_proj(x, P["norm_mix"][1], P["mla_w_in"], P["mla_q_norm"], P["mla_kv_norm"],
                            P["mla_wq"], P["mla_wk"], P["mla_wvt"], cos_tab, sin_tab)
    o = _mla_attention(q, k, vt)
    return _out_ffn(x, [o], P["mla_w_out"], P["norm_ffn"][1], *P["ffn"][1],
                    g_final=P["norm_final"])


def kernel(x_prompt, x_sample, norm_mix, norm_ffn, norm_final, ab_w_in, ab_conv_w, ab_conv_b, rg_w_a, rg_b_a, rg_w_i, rg_b_i, rg_lam, ab_w_out, mla_w_in, mla_q_norm, mla_w_qb, mla_kv_norm, mla_w_kvb, mla_w_out, ffn_w_gate, ffn_w_up, ffn_w_down):
    P = _prepare(norm_mix, norm_ffn, norm_final, ab_w_in, ab_conv_w, ab_conv_b, rg_w_a, rg_b_a,
                 rg_w_i, rg_b_i, rg_lam, ab_w_out, mla_w_in, mla_q_norm, mla_w_qb, mla_kv_norm,
                 mla_w_kvb, mla_w_out, ffn_w_gate, ffn_w_up, ffn_w_down)
    return (_trunk(x_prompt, P), _trunk(x_sample, P))
```

```python
import functools
import math

import jax
import jax.numpy as jnp
from jax import lax
from jax.experimental import pallas as pl
from jax.experimental.pallas import tpu as pltpu

F32 = jnp.float32
BF16 = jnp.bfloat16

D_MODEL = 1024
EPS = 1e-6
RG_WIDTH = 512
RG_BLOCKS = 8
RG_C = 8.0
DA_HEADS = 8
DA_HEAD_DIM = 64
DA_WIDTH = 512
DA_DILATIONS = (1, 4, 16)
DA_HALF_STEPS = 64
DA_TOKENS = DA_HALF_STEPS * max(DA_DILATIONS)
DA_Q_SLOT = 128
DA_K_SLOT = 256
MLA_HEADS = 16
MLA_Q_RANK = 384
MLA_KV_RANK = 256
MLA_NOPE = 64
MLA_ROPE = 32
MLA_V = 64
MLA_QK = MLA_NOPE + MLA_ROPE
MLA_ONES = 16
MLA_VROWS = MLA_V + MLA_ONES
MLA_MAX_LAG_EXCESS = 100.0
ROPE_THETA = 10000.0
FFN_HIDDEN = 2816
NEG_BIG = -1e30
LANES = 128

VMEM_LIMIT = 56 * 1024 * 1024

ROW_TILE = 512
SCAN_TILE = 512
SCAN_PAD = 8
MLA_Q_TILE = 1024
MLA_K_TILE = ROW_TILE


def _params(*sem):
    return pltpu.CompilerParams(dimension_semantics=sem, vmem_limit_bytes=VMEM_LIMIT)


def _const_spec(shape):
    nd = len(shape)
    return pl.BlockSpec(shape, lambda *_: (0,) * nd)


def _rms(x, g):
    return x * lax.rsqrt(jnp.mean(x * x, axis=-1, keepdims=True) + EPS) * g


def _ab_in_kernel(x_ref, g_ref, w_ref, xg_ref, qkv_ref):
    h = _rms(x_ref[...], g_ref[...]).astype(BF16)
    xg_ref[...] = jnp.dot(h, w_ref[:, :2 * RG_WIDTH], preferred_element_type=F32)
    n_slabs = 3 * DA_WIDTH // LANES
    qkv = jnp.dot(h, w_ref[:, 2 * RG_WIDTH:], preferred_element_type=F32)
    for j in range(n_slabs):
        qkv_ref[j] = qkv[:, LANES * j:LANES * (j + 1)]


def _ab_in_proj(x, g, w):
    B, S, D = x.shape
    n_slabs = 3 * DA_WIDTH // LANES
    tm = ROW_TILE
    return pl.pallas_call(
        _ab_in_kernel,
        grid=(B, S // tm),
        in_specs=[
            pl.BlockSpec((None, tm, D), lambda b, i: (b, i, 0)),
            _const_spec(g.shape),
            _const_spec(w.shape),
        ],
        out_specs=[
            pl.BlockSpec((None, tm, 2 * RG_WIDTH), lambda b, i: (b, i, 0)),
            pl.BlockSpec((None, n_slabs, tm, LANES), lambda b, i: (b, 0, i, 0)),
        ],
        out_shape=[
            jax.ShapeDtypeStruct((B, S, 2 * RG_WIDTH), F32),
            jax.ShapeDtypeStruct((B, n_slabs, S, LANES), F32),
        ],
        compiler_params=_params("parallel", "parallel"),
        name="ab_in_proj",
    )(x, g, w)


def _shift_rows(x, d, edge, toward_end):
    t = SCAN_PAD
    n = x.shape[0] // t
    sub = lax.broadcasted_iota(jnp.int32, (t, x.shape[1]), 0)
    tiles = [x[t * i:t * (i + 1)] for i in range(n)]
    if toward_end:
        rot = [pltpu.roll(b, d, 0) for b in [edge] + tiles]
        out = [jnp.where(sub < d, rot[i], rot[i + 1]) for i in range(n)]
    else:
        rot = [pltpu.roll(b, t - d, 0) for b in tiles + [edge]]
        out = [jnp.where(sub >= t - d, rot[i + 1], rot[i]) for i in range(n)]
    return jnp.concatenate(out, axis=0)


def _rglru_kernel(reverse, n_chunks, *refs):
    if reverse:
        (x_ref, xp_ref, xn_ref, gate_ref, hf_ref, cw_ref, cb_ref, wg_ref, bg_ref, lam_ref,
         out_ref, carry_ref) = refs
    else:
        (x_ref, xp_ref, xn_ref, cw_ref, cb_ref, wg_ref, bg_ref, lam_ref,
         out_ref, carry_ref) = refs
    i = pl.program_id(1)
    c = (n_chunks - 1 - i) if reverse else i
    tc, width = x_ref.shape

    @pl.when(i == 0)
    def _():
        carry_ref[...] = jnp.zeros_like(carry_ref)

    x = x_ref[...]
    prev = jnp.where(c > 0, xp_ref[...], 0.0)
    nxt = jnp.where(c < n_chunks - 1, xn_ref[...], 0.0)
    cw = cw_ref[...]
    xc = (cw[0:1] * _shift_rows(x, 2, prev, True) + cw[1:2] * _shift_rows(x, 1, prev, True)
          + cw[2:3] * x + cw[3:4] * _shift_rows(x, 1, nxt, False) + cb_ref[...])

    z = jnp.dot(xc.astype(BF16), wg_ref[...], preferred_element_type=F32) + bg_ref[...]
    gates = 0.5 * jnp.tanh(0.5 * z) + 0.5
    r = gates[:, :width]
    gi = gates[:, width:]
    log_a = (-RG_C) * r * jax.nn.softplus(-lam_ref[...])
    a = jnp.exp(log_a)
    z2 = -jnp.tanh(log_a) * (a * a + 1.0)
    u = jnp.where(z2 > 0.0, z2 * lax.rsqrt(z2), 0.0) * (gi * xc)

    one_edge = jnp.ones((SCAN_PAD, width), F32)
    zero_edge = jnp.zeros((SCAN_PAD, width), F32)
    d = 1
    while d < tc:
        if d < SCAN_PAD:
            a_sh = _shift_rows(a, d, one_edge, not reverse)
            u = a * _shift_rows(u, d, zero_edge, not reverse) + u
            a = a * a_sh
        elif reverse:
            u = jnp.concatenate([a[:tc - d] * u[d:] + u[:tc - d], u[tc - d:]], axis=0)
            a = jnp.concatenate([a[:tc - d] * a[d:], a[tc - d:]], axis=0)
        else:
            u = jnp.concatenate([u[:d], a[d:] * u[:tc - d] + u[d:]], axis=0)
            a = jnp.concatenate([a[:d], a[d:] * a[:tc - d]], axis=0)
        d *= 2
    h = u + a * carry_ref[...]
    carry_ref[...] = h[0:1] if reverse else h[tc - 1:tc]

    if reverse:
        out_ref[...] = (jax.nn.gelu(gate_ref[...]) * (hf_ref[...] + h)).astype(out_ref.dtype)
    else:
        out_ref[...] = h


def _rglru_pass(xg, hf, conv_w, conv_b, wg, bg, lam, reverse):
    B, S, _ = xg.shape
    tc = SCAN_TILE
    n_chunks = S // tc
    sub = SCAN_PAD

    def chunk(i):
        return (n_chunks - 1 - i) if reverse else i

    x_spec = pl.BlockSpec((None, tc, RG_WIDTH), lambda b, i: (b, chunk(i), 0))
    prev_spec = pl.BlockSpec(
        (None, sub, RG_WIDTH), lambda b, i: (b, jnp.maximum(chunk(i) * (tc // sub) - 1, 0), 0))
    next_spec = pl.BlockSpec(
        (None, sub, RG_WIDTH),
        lambda b, i: (b, jnp.minimum((chunk(i) + 1) * (tc // sub), S // sub - 1), 0))
    consts = [conv_w, conv_b, wg, bg, lam]
    const_specs = [_const_spec(t.shape) for t in consts]
    if reverse:
        gate_spec = pl.BlockSpec((None, tc, RG_WIDTH), lambda b, i: (b, chunk(i), 1))
        hf_spec = pl.BlockSpec((None, tc, RG_WIDTH), lambda b, i: (b, chunk(i), 0))
        inputs = [xg, xg, xg, xg, hf] + consts
        in_specs = [x_spec, prev_spec, next_spec, gate_spec, hf_spec] + const_specs
        out_dtype = BF16
    else:
        inputs = [xg, xg, xg] + consts
        in_specs = [x_spec, prev_spec, next_spec] + const_specs
        out_dtype = F32
    return pl.pallas_call(
        functools.partial(_rglru_kernel, reverse, n_chunks),
        grid=(B, n_chunks),
        in_specs=in_specs,
        out_specs=pl.BlockSpec((None, tc, RG_WIDTH), lambda b, i: (b, chunk(i), 0)),
        out_shape=jax.ShapeDtypeStruct((B, S, RG_WIDTH), out_dtype),
        scratch_shapes=[pltpu.VMEM((1, RG_WIDTH), F32)],
        compiler_params=_params("parallel", "arbitrary"),
        name="rglru_bwd" if reverse else "rglru_fwd",
    )(*inputs)


def _dilated_kernel(n_blocks, slopes, q_ref, kp_ref, kc_ref, kn_ref, vp_ref, vc_ref, vn_ref,
                    out_ref, qs_ref, ks_ref, vs_ref, o_scr, lse_scr, bias_scr):
    i = pl.program_id(1)
    T = DA_TOKENS
    W = DA_HALF_STEPS
    n_pairs = DA_WIDTH // LANES
    log2e = math.log2(math.e)
    scale = log2e / math.sqrt(DA_HEAD_DIM)
    is_first = i == 0
    is_last = i == n_blocks - 1

    for g, dil in enumerate(DA_DILATIONS):
        n = T // dil
        bq = min(n, 128)
        bk = bq + 2 * W
        rowq = lax.broadcasted_iota(jnp.int32, (bq, bk), 0)
        colk = lax.broadcasted_iota(jnp.int32, (bq, bk), 1)
        rel = colk - W - rowq
        band = jnp.abs(rel) <= W
        neg_dist = -(jnp.abs(rel) * dil).astype(F32)
        lane = lax.broadcasted_iota(jnp.int32, (bq, LANES), 1)
        lo_lane = lane < DA_HEAD_DIM
        for head in range(DA_HEADS):
            bias_scr[head, 0:bq, 0:bk] = jnp.where(band, (slopes[head] * log2e) * neg_dist,
                                                   NEG_BIG)

        def stage(r, slot, dil=dil, n=n):
            qo, ko = slot * DA_Q_SLOT, slot * DA_K_SLOT
            for p in range(n_pairs):
                qs_ref[p, qo:qo + n] = (q_ref[p, pl.ds(r, n, stride=dil), :] * scale).astype(BF16)
                for src_p, src_c, src_n, dst in ((kp_ref, kc_ref, kn_ref, ks_ref),
                                                 (vp_ref, vc_ref, vn_ref, vs_ref)):
                    dst[p, ko:ko + W] = src_p[p, pl.ds(r + dil * (n - W), W, stride=dil),
                                              :].astype(BF16)
                    dst[p, ko + W:ko + W + n] = src_c[p, pl.ds(r, n, stride=dil), :].astype(BF16)
                    dst[p, ko + W + n:ko + 2 * W + n] = src_n[p, pl.ds(r, W, stride=dil),
                                                             :].astype(BF16)

        def attend(items, dil=dil, n=n, bq=bq, bk=bk, g=g, colk=colk, lo_lane=lo_lane):
            def start(q0, slot, pitch):
                return q0 if slot == 0 else slot * pitch + q0

            scores = []
            for r, q0, slot in items:
                for p in range(n_pairs):
                    qp = qs_ref[p, pl.ds(start(q0, slot, DA_Q_SLOT), bq), :]
                    kp = ks_ref[p, pl.ds(start(q0, slot, DA_K_SLOT), bk), :]
                    for hh in range(2):
                        qh = jnp.where(lo_lane if hh == 0 else jnp.logical_not(lo_lane), qp,
                                       jnp.zeros_like(qp))
                        scores.append(lax.dot_general(qh, kp, (((1,), (1,)), ((), ())),
                                                      preferred_element_type=F32))
            probs, lse2, inv_l = [], [], []
            for idx, s in enumerate(scores):
                r, q0, slot = items[idx // DA_HEADS]
                pos = q0 + colk
                in_seq = jnp.logical_not(
                    jnp.logical_or(jnp.logical_and(is_first, pos < W),
                                   jnp.logical_and(is_last, pos >= n + W)))
                s = jnp.where(in_seq, s + bias_scr[idx % DA_HEADS, 0:bq, 0:bk], NEG_BIG)
                m = jnp.max(s, axis=-1, keepdims=True)
                e = jnp.exp2(s - m)
                l = jnp.sum(e, axis=-1, keepdims=True)
                probs.append(e.astype(BF16))
                inv_l.append(1.0 / l)
                lse2.append(m + jnp.log2(l))
            for it, (r, q0, slot) in enumerate(items):
                for p in range(n_pairs):
                    vp = vs_ref[p, pl.ds(start(q0, slot, DA_K_SLOT), bk), :]
                    h0 = it * DA_HEADS + 2 * p
                    o_h = [jnp.dot(probs[h0 + hh], vp, preferred_element_type=F32)
                           * inv_l[h0 + hh] for hh in range(2)]
                    rows = pl.ds(r + dil * q0, bq, stride=dil)
                    o_scr[g, p, rows, :] = jnp.where(lo_lane, o_h[0], o_h[1])
                    lse_scr[g, p, rows, :] = jnp.where(lo_lane, lse2[h0], lse2[h0 + 1])

        if n == bq:
            def class_pair(t, carry, stage=stage, attend=attend):
                stage(2 * t, 0)
                stage(2 * t + 1, 1)
                attend([(2 * t, 0, 0), (2 * t + 1, 0, 1)])
                return carry

            lax.fori_loop(0, dil // 2, class_pair, 0)
        else:
            def class_body(r, carry, stage=stage, attend=attend, bq=bq, n=n):
                stage(r, 0)

                def qb_pair(t, carry2):
                    q0 = pl.multiple_of(t * (2 * bq), 2 * bq)
                    attend([(r, q0, 0), (r, pl.multiple_of(q0 + bq, bq), 0)])
                    return carry2

                lax.fori_loop(0, n // (2 * bq), qb_pair, 0)
                return carry

            lax.fori_loop(0, dil, class_body, 0)

    rows_per_step = 256

    def merge_body(t, carry):
        t0 = pl.multiple_of(t * rows_per_step, rows_per_step)
        rows = pl.ds(t0, rows_per_step)
        for p in range(n_pairs):
            lse = [lse_scr[g, p, rows, :] for g in range(len(DA_DILATIONS))]
            mx = jnp.maximum(jnp.maximum(lse[0], lse[1]), lse[2])
            w = [jnp.exp2(v - mx) for v in lse]
            num = sum(w[g] * o_scr[g, p, rows, :] for g in range(len(DA_DILATIONS)))
            out_ref[rows, LANES * p:LANES * (p + 1)] = (num / (w[0] + w[1] + w[2])).astype(
                out_ref.dtype)
        return carry

    lax.fori_loop(0, T // rows_per_step, merge_body, 0)


def _dilated_attention(qkv):
    B, _, S, _ = qkv.shape
    T = DA_TOKENS
    n_blocks = S // T
    n_pairs = DA_WIDTH // LANES
    slopes = tuple(2.0 ** (-8.0 * (h + 1) / DA_HEADS) for h in range(DA_HEADS))

    def spec(slab, shift):
        def index(b, i):
            return (b, slab, jnp.clip(i + shift, 0, n_blocks - 1), 0)
        return pl.BlockSpec((None, n_pairs, T, LANES), index)

    win = T + 2 * DA_HALF_STEPS
    return pl.pallas_call(
        functools.partial(_dilated_kernel, n_blocks, slopes),
        grid=(B, n_blocks),
        in_specs=[spec(0, 0), spec(1, -1), spec(1, 0), spec(1, 1),
                  spec(2, -1), spec(2, 0), spec(2, 1)],
        out_specs=pl.BlockSpec((None, T, DA_WIDTH), lambda b, i: (b, i, 0)),
        out_shape=jax.ShapeDtypeStruct((B, S, DA_WIDTH), BF16),
        scratch_shapes=[
            pltpu.VMEM((n_pairs, T, LANES), BF16),
            pltpu.VMEM((n_pairs, win, LANES), BF16),
            pltpu.VMEM((n_pairs, win, LANES), BF16),
            pltpu.VMEM((len(DA_DILATIONS), n_pairs, T, LANES), F32),
            pltpu.VMEM((len(DA_DILATIONS), n_pairs, T, LANES), F32),
            pltpu.VMEM((DA_HEADS, 128, 128 + 2 * DA_HALF_STEPS), F32),
        ],
        compiler_params=_params("parallel", "parallel"),
        name="dilated_attention",
    )(qkv, qkv, qkv, qkv, qkv, qkv, qkv)


def _out_ffn_kernel(n_y, final_norm, *refs):
    x_ref = refs[0]
    y_refs = refs[1:1 + n_y]
    wo_ref, g_ref, wg_ref, wu_ref, wd_ref = refs[1 + n_y:6 + n_y]
    gf_ref = refs[6 + n_y] if final_norm else None
    out_ref = refs[-1]

    y = y_refs[0][...] if n_y == 1 else jnp.concatenate([r[...] for r in y_refs], axis=1)
    x = x_ref[...] + jnp.dot(y, wo_ref[...], preferred_element_type=F32)
    h = _rms(x, g_ref[...]).astype(BF16)
    gate = jnp.dot(h, wg_ref[...], preferred_element_type=F32)
    up = jnp.dot(h, wu_ref[...], preferred_element_type=F32)
    act = (jax.nn.silu(gate) * up).astype(BF16)
    x = x + jnp.dot(act, wd_ref[...], preferred_element_type=F32)
    if final_norm:
        x = _rms(x, gf_ref[...])
    out_ref[...] = x


def _out_ffn(x, ys, w_out, g, wg, wu, wd, g_final=None):
    B, S, D = x.shape
    rows = B * S
    tm = ROW_TILE
    x2 = x.reshape(rows, D)
    ys2 = [y.reshape(rows, y.shape[-1]) for y in ys]
    consts = [w_out, g, wg, wu, wd] + ([g_final] if g_final is not None else [])
    out = pl.pallas_call(
        functools.partial(_out_ffn_kernel, len(ys), g_final is not None),
        grid=(rows // tm,),
        in_specs=([pl.BlockSpec((tm, D), lambda i: (i, 0))]
                  + [pl.BlockSpec((tm, y.shape[-1]), lambda i: (i, 0)) for y in ys2]
                  + [pl.BlockSpec(t.shape, lambda i, nd=t.ndim: (0,) * nd,
                                  pipeline_mode=pl.Buffered(1)) for t in consts]),
        out_specs=pl.BlockSpec((tm, D), lambda i: (i, 0)),
        out_shape=jax.ShapeDtypeStruct((rows, D), F32),
        compiler_params=_params("parallel"),
        name="out_proj_ffn",
    )(x2, *ys2, *consts)
    return out.reshape(B, S, D)


def _mla_in_kernel(x_ref, g_ref, win_ref, qn_ref, kvn_ref, wq_ref, wk_ref, wvt_ref, cos_ref,
                   sin_ref, q_ref, k_ref, vt_ref):
    h = _rms(x_ref[...], g_ref[...]).astype(BF16)
    proj = jnp.dot(h, win_ref[...], preferred_element_type=F32)
    cq = _rms(proj[:, :MLA_Q_RANK], qn_ref[...]).astype(BF16)
    ckv_f32 = _rms(proj[:, MLA_Q_RANK:MLA_Q_RANK + MLA_KV_RANK], kvn_ref[...])
    ckv = ckv_f32.astype(BF16)
    base = MLA_Q_RANK + MLA_KV_RANK
    cos = cos_ref[...]
    sin = sin_ref[...]
    k_rope = proj[:, base:base + LANES] * cos + proj[:, base + LANES:base + 2 * LANES] * sin
    scale = math.log2(math.e) / math.sqrt(MLA_QK)
    for hd in range(MLA_HEADS):
        cols = slice(hd * LANES, (hd + 1) * LANES)
        qq = jnp.dot(cq, wq_ref[:, 2 * hd * LANES:2 * (hd + 1) * LANES],
                     preferred_element_type=F32)
        q_ref[:, cols] = ((qq[:, :LANES] * cos + qq[:, LANES:] * sin) * scale).astype(BF16)
    for pair in range(MLA_HEADS // 2):
        cols = slice(2 * pair * LANES, 2 * (pair + 1) * LANES)
        kn = jnp.dot(ckv, wk_ref[:, cols], preferred_element_type=F32)
        k_ref[:, cols] = (kn + jnp.concatenate([k_rope, k_rope], axis=1)).astype(BF16)
    vt = jnp.dot(wvt_ref[...], ckv_f32.T.astype(BF16), preferred_element_type=F32)
    ones = jnp.ones((MLA_ONES, vt.shape[1]), BF16)
    for hd in range(MLA_HEADS):
        vt_ref[hd * MLA_VROWS:hd * MLA_VROWS + MLA_V, :] = (
            vt[hd * MLA_V:(hd + 1) * MLA_V].astype(BF16))
        vt_ref[hd * MLA_VROWS + MLA_V:(hd + 1) * MLA_VROWS, :] = ones


def _mla_in_proj(x, g, win, qn, kvn, wq, wk, wvt, cos_tab, sin_tab):
    B, S, D = x.shape
    tm = ROW_TILE
    width = MLA_HEADS * LANES
    consts = [g, win, qn, kvn, wq, wk, wvt]
    return pl.pallas_call(
        _mla_in_kernel,
        grid=(B, S // tm),
        in_specs=([pl.BlockSpec((None, tm, D), lambda b, i: (b, i, 0))]
                  + [_const_spec(t.shape) for t in consts]
                  + [pl.BlockSpec((tm, LANES), lambda b, i: (i, 0)),
                     pl.BlockSpec((tm, LANES), lambda b, i: (i, 0))]),
        out_specs=[
            pl.BlockSpec((None, tm, width), lambda b, i: (b, i, 0)),
            pl.BlockSpec((None, tm, width), lambda b, i: (b, i, 0)),
            pl.BlockSpec((None, None, MLA_HEADS * MLA_VROWS, tm), lambda b, i: (b, i, 0, 0)),
        ],
        out_shape=[
            jax.ShapeDtypeStruct((B, S, width), BF16),
            jax.ShapeDtypeStruct((B, S, width), BF16),
            jax.ShapeDtypeStruct((B, S // tm, MLA_HEADS * MLA_VROWS, tm), BF16),
        ],
        compiler_params=_params("parallel", "parallel"),
        name="mla_in_proj",
    )(x, *consts, cos_tab, sin_tab)


def _mla_attn_kernel(q_ref, k_ref, vt_ref, o_ref, acc_scr):
    n_kb, _, bk = vt_ref.shape
    bq = q_ref.shape[0]
    qs = [q_ref[:, hh * LANES:(hh + 1) * LANES] for hh in range(2)]
    rows = vt_ref.shape[1] // 2
    nt = (((1,), (1,)), ((), ()))

    def scores(j, hh):
        return lax.dot_general(k_ref[j * bk:(j + 1) * bk, hh * LANES:(hh + 1) * LANES], qs[hh],
                               nt, preferred_element_type=F32)

    def write_output():
        outs = [acc_scr[hh, 0:MLA_V] / acc_scr[hh, MLA_V:MLA_V + 1] for hh in range(2)]
        o_ref[...] = jnp.concatenate(outs, axis=0).T.astype(o_ref.dtype)

    ref_max = [None, None]
    run_max = [None, None]
    excess = jnp.zeros((1, bq), F32)
    sts = [scores(0, hh) for hh in range(2)]
    for j in range(n_kb):
        for hh in range(2):
            blk_max = jnp.max(sts[hh], axis=0, keepdims=True)
            if j == 0:
                new_ref, alpha = blk_max, None
                run_max[hh] = blk_max
            else:
                new_ref = run_max[hh]
                alpha = jnp.exp2(ref_max[hh] - new_ref)
                excess = jnp.maximum(excess, blk_max - new_ref)
                run_max[hh] = jnp.maximum(run_max[hh], blk_max)
            ref_max[hh] = new_ref
            p = jnp.exp2(sts[hh] - new_ref).astype(BF16)
            if j + 1 < n_kb:
                sts[hh] = scores(j + 1, hh)
            pv = jnp.dot(vt_ref[j, hh * rows:(hh + 1) * rows, :], p, preferred_element_type=F32)
            acc_scr[hh] = pv if alpha is None else alpha * acc_scr[hh] + pv
    write_output()

    @pl.when(jnp.max(excess) > MLA_MAX_LAG_EXCESS)
    def _():
        acc_scr[...] = jnp.zeros_like(acc_scr)

        def body(j, ms):
            k0 = pl.multiple_of(j * bk, bk)
            new = []
            for hh in range(2):
                st = lax.dot_general(k_ref[pl.ds(k0, bk), hh * LANES:(hh + 1) * LANES], qs[hh],
                                     nt, preferred_element_type=F32)
                m_new = jnp.maximum(ms[hh], jnp.max(st, axis=0, keepdims=True))
                p = jnp.exp2(st - m_new).astype(BF16)
                pv = jnp.dot(vt_ref[j, hh * rows:(hh + 1) * rows, :], p,
                             preferred_element_type=F32)
                acc_scr[hh] = jnp.exp2(ms[hh] - m_new) * acc_scr[hh] + pv
                new.append(m_new)
            return tuple(new)

        lax.fori_loop(0, n_kb, body, tuple(jnp.full((1, bq), NEG_BIG, F32) for _ in range(2)))
        write_output()


def _mla_attention(q, k, vt):
    B, S, _ = q.shape
    n_kb = vt.shape[1]
    bq = MLA_Q_TILE
    n_pairs = MLA_HEADS // 2
    return pl.pallas_call(
        _mla_attn_kernel,
        grid=(B, n_pairs, S // bq),
        in_specs=[
            pl.BlockSpec((None, bq, 2 * LANES), lambda b, p, i: (b, i, p)),
            pl.BlockSpec((None, S, 2 * LANES), lambda b, p, i: (b, 0, p)),
            pl.BlockSpec((None, n_kb, 2 * MLA_VROWS, MLA_K_TILE), lambda b, p, i: (b, 0, p, 0)),
        ],
        out_specs=pl.BlockSpec((None, bq, 2 * MLA_V), lambda b, p, i: (b, i, p)),
        out_shape=jax.ShapeDtypeStruct((B, S, MLA_HEADS * MLA_V), BF16),
        scratch_shapes=[
            pltpu.VMEM((2, MLA_VROWS, bq), F32),
        ],
        compiler_params=_params("parallel", "parallel", "arbitrary"),
        name="mla_attention",
    )(q, k, vt)


def _block_diag(w):
    nb, d, _ = w.shape
    eye = jnp.eye(nb, dtype=w.dtype)
    return (eye[:, None, :, None] * w[:, :, None, :]).reshape(nb * d, nb * d)


def _rot_cols(w):
    half = MLA_ROPE // 2
    return jnp.concatenate([-w[..., half:], w[..., :half]], axis=-1)


def _pad_lanes(w, offset):
    return jnp.pad(w, [(0, 0)] * (w.ndim - 1) + [(offset, LANES - offset - w.shape[-1])])


def _prepare(norm_mix, norm_ffn, norm_final, ab_w_in, ab_conv_w, ab_conv_b, rg_w_a, rg_b_a,
             rg_w_i, rg_b_i, rg_lam, ab_w_out, mla_w_in, mla_q_norm, mla_w_qb, mla_kv_norm,
             mla_w_kvb, mla_w_out, ffn_w_gate, ffn_w_up, ffn_w_down):
    row = lambda v: v.reshape(1, -1).astype(F32)
    P = {}
    P["norm_mix"] = [row(norm_mix[l]) for l in range(2)]
    P["norm_ffn"] = [row(norm_ffn[l]) for l in range(2)]
    P["norm_final"] = row(norm_final)
    P["ab_w_in"] = ab_w_in[0].astype(BF16)
    P["conv_w"] = ab_conv_w[0].astype(F32)
    P["conv_b"] = row(ab_conv_b[0])
    P["rg_w"] = [jnp.concatenate([_block_diag(rg_w_a[0, d]), _block_diag(rg_w_i[0, d])],
                                 axis=1).astype(BF16) for d in range(2)]
    P["rg_b"] = [jnp.concatenate([rg_b_a[0, d], rg_b_i[0, d]]).reshape(1, -1) for d in range(2)]
    P["rg_lam"] = [row(rg_lam[0, d]) for d in range(2)]
    P["ab_w_out"] = ab_w_out[0].astype(BF16)

    w_in = mla_w_in[0]
    base = MLA_Q_RANK + MLA_KV_RANK
    w_kr = w_in[:, base:]
    P["mla_w_in"] = jnp.concatenate(
        [w_in[:, :base], _pad_lanes(w_kr, MLA_NOPE), _pad_lanes(_rot_cols(w_kr), MLA_NOPE)],
        axis=1).astype(BF16)
    P["mla_q_norm"] = row(mla_q_norm[0])
    P["mla_kv_norm"] = row(mla_kv_norm[0])
    wq = mla_w_qb[0].reshape(MLA_Q_RANK, MLA_HEADS, MLA_QK)
    wq_plain = _pad_lanes(wq, 0)
    wq_rot = _pad_lanes(_rot_cols(wq[..., MLA_NOPE:]), MLA_NOPE)
    P["mla_wq"] = jnp.concatenate([wq_plain, wq_rot], axis=-1).reshape(MLA_Q_RANK, -1).astype(BF16)
    wkv = mla_w_kvb[0].reshape(MLA_KV_RANK, MLA_HEADS, MLA_NOPE + MLA_V)
    P["mla_wk"] = _pad_lanes(wkv[..., :MLA_NOPE], 0).reshape(MLA_KV_RANK, -1).astype(BF16)
    P["mla_wvt"] = wkv[..., MLA_NOPE:].reshape(MLA_KV_RANK, -1).T.astype(BF16)
    P["mla_w_out"] = mla_w_out[0].astype(BF16)
    P["ffn"] = [(ffn_w_gate[l].astype(BF16), ffn_w_up[l].astype(BF16),
                 ffn_w_down[l].astype(BF16)) for l in range(2)]
    return P


def _rope_slabs(S):
    inv_freq = 1.0 / (ROPE_THETA ** (jnp.arange(0, MLA_ROPE, 2, dtype=F32) / MLA_ROPE))
    ang = jnp.arange(S, dtype=F32)[:, None] * inv_freq[None, :]
    cos, sin = jnp.cos(ang), jnp.sin(ang)
    pad = LANES - MLA_QK
    cos_tab = jnp.concatenate([jnp.ones((S, MLA_NOPE), F32), cos, cos, jnp.zeros((S, pad), F32)], 1)
    sin_tab = jnp.concatenate([jnp.zeros((S, MLA_NOPE), F32), sin, sin, jnp.zeros((S, pad), F32)], 1)
    return cos_tab, sin_tab


def _trunk(x, P):
    S = x.shape[1]
    xg, qkv = _ab_in_proj(x, P["norm_mix"][0], P["ab_w_in"])
    hf = _rglru_pass(xg, None, P["conv_w"], P["conv_b"], P["rg_w"][0], P["rg_b"][0],
                     P["rg_lam"][0], reverse=False)
    y_rnn = _rglru_pass(xg, hf, P["conv_w"], P["conv_b"], P["rg_w"][1], P["rg_b"][1],
                        P["rg_lam"][1], reverse=True)
    o = _dilated_attention(qkv)
    x = _out_ffn(x, [y_rnn, o], P["ab_w_out"], P["norm_ffn"][0], *P["ffn"][0])
    cos_tab, sin_tab = _rope_slabs(S)
    q, k, vt = _mla_in_proj(x, P["norm_mix"][1], P["mla_w_in"], P["mla_q_norm"], P["mla_kv_norm"],
                            P["mla_wq"], P["mla_wk"], P["mla_wvt"], cos_tab, sin_tab)
    o = _mla_attention(q, k, vt)
    return _out_ffn(x, [o], P["mla_w_out"], P["norm_ffn"][1], *P["ffn"][1],
                    g_final=P["norm_final"])


def kernel(x_prompt, x_sample, norm_mix, norm_ffn, norm_final, ab_w_in, ab_conv_w, ab_conv_b, rg_w_a, rg_b_a, rg_w_i, rg_b_i, rg_lam, ab_w_out, mla_w_in, mla_q_norm, mla_w_qb, mla_kv_norm, mla_w_kvb, mla_w_out, ffn_w_gate, ffn_w_up, ffn_w_down):
    P = _prepare(norm_mix, norm_ffn, norm_final, ab_w_in, ab_conv_w, ab_conv_b, rg_w_a, rg_b_a,
                 rg_w_i, rg_b_i, rg_lam, ab_w_out, mla_w_in, mla_q_norm, mla_w_qb, mla_kv_norm,
                 mla_w_kvb, mla_w_out, ffn_w_gate, ffn_w_up, ffn_w_down)
    return (_trunk(x_prompt, P), _trunk(x_sample, P))
```

```python
import functools
import math

import jax
import jax.numpy as jnp
from jax import lax
from jax.experimental import pallas as pl
from jax.experimental.pallas import tpu as pltpu

F32 = jnp.float32
BF16 = jnp.bfloat16

D_MODEL = 1024
EPS = 1e-6
RG_WIDTH = 512
RG_BLOCKS = 8
RG_C = 8.0
DA_HEADS = 8
DA_HEAD_DIM = 64
DA_WIDTH = 512
DA_DILATIONS = (1, 4, 16)
DA_HALF_STEPS = 64
DA_TOKENS = DA_HALF_STEPS * max(DA_DILATIONS)
DA_Q_SLOT = 128
DA_K_SLOT = 256
MLA_HEADS = 16
MLA_Q_RANK = 384
MLA_KV_RANK = 256
MLA_NOPE = 64
MLA_ROPE = 32
MLA_V = 64
MLA_QK = MLA_NOPE + MLA_ROPE
MLA_ONES = 16
MLA_VROWS = MLA_V + MLA_ONES
MLA_MAX_LAG_EXCESS = 100.0
ROPE_THETA = 10000.0
FFN_HIDDEN = 2816
NEG_BIG = -1e30
LANES = 128

VMEM_LIMIT = 56 * 1024 * 1024

ROW_TILE = 512
SCAN_TILE = 512
SCAN_PAD = 8
MLA_Q_TILE = 1024
MLA_K_TILE = ROW_TILE


def _params(*sem):
    return pltpu.CompilerParams(dimension_semantics=sem, vmem_limit_bytes=VMEM_LIMIT)


def _const_spec(shape):
    nd = len(shape)
    return pl.BlockSpec(shape, lambda *_: (0,) * nd)


def _rms(x, g):
    return x * lax.rsqrt(jnp.mean(x * x, axis=-1, keepdims=True) + EPS) * g


def _ab_in_kernel(x_ref, g_ref, w_ref, xg_ref, qkv_ref):
    h = _rms(x_ref[...], g_ref[...]).astype(BF16)
    xg_ref[...] = jnp.dot(h, w_ref[:, :2 * RG_WIDTH], preferred_element_type=F32)
    n_slabs = 3 * DA_WIDTH // LANES
    qkv = jnp.dot(h, w_ref[:, 2 * RG_WIDTH:], preferred_element_type=F32)
    for j in range(n_slabs):
        qkv_ref[j] = qkv[:, LANES * j:LANES * (j + 1)]


def _ab_in_proj(x, g, w):
    B, S, D = x.shape
    n_slabs = 3 * DA_WIDTH // LANES
    tm = ROW_TILE
    return pl.pallas_call(
        _ab_in_kernel,
        grid=(B, S // tm),
        in_specs=[
            pl.BlockSpec((None, tm, D), lambda b, i: (b, i, 0)),
            _const_spec(g.shape),
            _const_spec(w.shape),
        ],
        out_specs=[
            pl.BlockSpec((None, tm, 2 * RG_WIDTH), lambda b, i: (b, i, 0)),
            pl.BlockSpec((None, n_slabs, tm, LANES), lambda b, i: (b, 0, i, 0)),
        ],
        out_shape=[
            jax.ShapeDtypeStruct((B, S, 2 * RG_WIDTH), F32),
            jax.ShapeDtypeStruct((B, n_slabs, S, LANES), F32),
        ],
        compiler_params=_params("parallel", "parallel"),
        name="ab_in_proj",
    )(x, g, w)


def _shift_rows(x, d, edge, toward_end):
    t = SCAN_PAD
    n = x.shape[0] // t
    sub = lax.broadcasted_iota(jnp.int32, (t, x.shape[1]), 0)
    tiles = [x[t * i:t * (i + 1)] for i in range(n)]
    if toward_end:
        rot = [pltpu.roll(b, d, 0) for b in [edge] + tiles]
        out = [jnp.where(sub < d, rot[i], rot[i + 1]) for i in range(n)]
    else:
        rot = [pltpu.roll(b, t - d, 0) for b in tiles + [edge]]
        out = [jnp.where(sub >= t - d, rot[i + 1], rot[i]) for i in range(n)]
    return jnp.concatenate(out, axis=0)


def _rglru_kernel(reverse, n_chunks, *refs):
    if reverse:
        (x_ref, xp_ref, xn_ref, gate_ref, hf_ref, cw_ref, cb_ref, wg_ref, bg_ref, lam_ref,
         out_ref, carry_ref) = refs
    else:
        (x_ref, xp_ref, xn_ref, cw_ref, cb_ref, wg_ref, bg_ref, lam_ref,
         out_ref, carry_ref) = refs
    i = pl.program_id(1)
    c = (n_chunks - 1 - i) if reverse else i
    tc, width = x_ref.shape

    @pl.when(i == 0)
    def _():
        carry_ref[...] = jnp.zeros_like(carry_ref)

    x = x_ref[...]
    prev = jnp.where(c > 0, xp_ref[...], 0.0)
    nxt = jnp.where(c < n_chunks - 1, xn_ref[...], 0.0)
    cw = cw_ref[...]
    xc = (cw[0:1] * _shift_rows(x, 2, prev, True) + cw[1:2] * _shift_rows(x, 1, prev, True)
          + cw[2:3] * x + cw[3:4] * _shift_rows(x, 1, nxt, False) + cb_ref[...])

    z = jnp.dot(xc.astype(BF16), wg_ref[...], preferred_element_type=F32) + bg_ref[...]
    gates = 0.5 * jnp.tanh(0.5 * z) + 0.5
    r = gates[:, :width]
    gi = gates[:, width:]
    log_a = (-RG_C) * r * jax.nn.softplus(-lam_ref[...])
    a = jnp.exp(log_a)
    z2 = -jnp.tanh(log_a) * (a * a + 1.0)
    u = jnp.where(z2 > 0.0, z2 * lax.rsqrt(z2), 0.0) * (gi * xc)

    one_edge = jnp.ones((SCAN_PAD, width), F32)
    zero_edge = jnp.zeros((SCAN_PAD, width), F32)
    d = 1
    while d < tc:
        if d < SCAN_PAD:
            a_sh = _shift_rows(a, d, one_edge, not reverse)
            u = a * _shift_rows(u, d, zero_edge, not reverse) + u
            a = a * a_sh
        elif reverse:
            u = jnp.concatenate([a[:tc - d] * u[d:] + u[:tc - d], u[tc - d:]], axis=0)
            a = jnp.concatenate([a[:tc - d] * a[d:], a[tc - d:]], axis=0)
        else:
            u = jnp.concatenate([u[:d], a[d:] * u[:tc - d] + u[d:]], axis=0)
            a = jnp.concatenate([a[:d], a[d:] * a[:tc - d]], axis=0)
        d *= 2
    h = u + a * carry_ref[...]
    carry_ref[...] = h[0:1] if reverse else h[tc - 1:tc]

    if reverse:
        out_ref[...] = (jax.nn.gelu(gate_ref[...]) * (hf_ref[...] + h)).astype(out_ref.dtype)
    else:
        out_ref[...] = h


def _rglru_pass(xg, hf, conv_w, conv_b, wg, bg, lam, reverse):
    B, S, _ = xg.shape
    tc = SCAN_TILE
    n_chunks = S // tc
    sub = SCAN_PAD

    def chunk(i):
        return (n_chunks - 1 - i) if reverse else i

    x_spec = pl.BlockSpec((None, tc, RG_WIDTH), lambda b, i: (b, chunk(i), 0))
    prev_spec = pl.BlockSpec(
        (None, sub, RG_WIDTH), lambda b, i: (b, jnp.maximum(chunk(i) * (tc // sub) - 1, 0), 0))
    next_spec = pl.BlockSpec(
        (None, sub, RG_WIDTH),
        lambda b, i: (b, jnp.minimum((chunk(i) + 1) * (tc // sub), S // sub - 1), 0))
    consts = [conv_w, conv_b, wg, bg, lam]
    const_specs = [_const_spec(t.shape) for t in consts]
    if reverse:
        gate_spec = pl.BlockSpec((None, tc, RG_WIDTH), lambda b, i: (b, chunk(i), 1))
        hf_spec = pl.BlockSpec((None, tc, RG_WIDTH), lambda b, i: (b, chunk(i), 0))
        inputs = [xg, xg, xg, xg, hf] + consts
        in_specs = [x_spec, prev_spec, next_spec, gate_spec, hf_spec] + const_specs
        out_dtype = BF16
    else:
        inputs = [xg, xg, xg] + consts
        in_specs = [x_spec, prev_spec, next_spec] + const_specs
        out_dtype = F32
    return pl.pallas_call(
        functools.partial(_rglru_kernel, reverse, n_chunks),
        grid=(B, n_chunks),
        in_specs=in_specs,
        out_specs=pl.BlockSpec((None, tc, RG_WIDTH), lambda b, i: (b, chunk(i), 0)),
        out_shape=jax.ShapeDtypeStruct((B, S, RG_WIDTH), out_dtype),
        scratch_shapes=[pltpu.VMEM((1, RG_WIDTH), F32)],
        compiler_params=_params("parallel", "arbitrary"),
        name="rglru_bwd" if reverse else "rglru_fwd",
    )(*inputs)


def _dilated_kernel(n_blocks, slopes, q_ref, kp_ref, kc_ref, kn_ref, vp_ref, vc_ref, vn_ref,
                    out_ref, qs_ref, ks_ref, vs_ref, o_scr, lse_scr, bias_scr):
    i = pl.program_id(1)
    T = DA_TOKENS
    W = DA_HALF_STEPS
    n_pairs = DA_WIDTH // LANES
    log2e = math.log2(math.e)
    scale = log2e / math.sqrt(DA_HEAD_DIM)
    is_first = i == 0
    is_last = i == n_blocks - 1

    for g, dil in enumerate(DA_DILATIONS):
        n = T // dil
        bq = min(n, 128)
        bk = bq + 2 * W
        rowq = lax.broadcasted_iota(jnp.int32, (bq, bk), 0)
        colk = lax.broadcasted_iota(jnp.int32, (bq, bk), 1)
        rel = colk - W - rowq
        band = jnp.abs(rel) <= W
        neg_dist = -(jnp.abs(rel) * dil).astype(F32)
        lane = lax.broadcasted_iota(jnp.int32, (bq, LANES), 1)
        lo_lane = lane < DA_HEAD_DIM
        for head in range(DA_HEADS):
            bias_scr[head, 0:bq, 0:bk] = jnp.where(band, (slopes[head] * log2e) * neg_dist,
                                                   NEG_BIG)

        def stage(r, slot, dil=dil, n=n):
            qo, ko = slot * DA_Q_SLOT, slot * DA_K_SLOT
            for p in range(n_pairs):
                qs_ref[p, qo:qo + n] = (q_ref[p, pl.ds(r, n, stride=dil), :] * scale).astype(BF16)
                for src_p, src_c, src_n, dst in ((kp_ref, kc_ref, kn_ref, ks_ref),
                                                 (vp_ref, vc_ref, vn_ref, vs_ref)):
                    dst[p, ko:ko + W] = src_p[p, pl.ds(r + dil * (n - W), W, stride=dil),
                                              :].astype(BF16)
                    dst[p, ko + W:ko + W + n] = src_c[p, pl.ds(r, n, stride=dil), :].astype(BF16)
                    dst[p, ko + W + n:ko + 2 * W + n] = src_n[p, pl.ds(r, W, stride=dil),
                                                             :].astype(BF16)

        def attend(items, dil=dil, n=n, bq=bq, bk=bk, g=g, colk=colk, lo_lane=lo_lane):
            def start(q0, slot, pitch):
                return q0 if slot == 0 else slot * pitch + q0

            scores = []
            for r, q0, slot in items:
                for p in range(n_pairs):
                    qp = qs_ref[p, pl.ds(start(q0, slot, DA_Q_SLOT), bq), :]
                    kp = ks_ref[p, pl.ds(start(q0, slot, DA_K_SLOT), bk), :]
                    for hh in range(2):
                        qh = jnp.where(lo_lane if hh == 0 else jnp.logical_not(lo_lane), qp,
                                       jnp.zeros_like(qp))
                        scores.append(lax.dot_general(qh, kp, (((1,), (1,)), ((), ())),
                                                      preferred_element_type=F32))
            probs, lse2, inv_l = [], [], []
            for idx, s in enumerate(scores):
                r, q0, slot = items[idx // DA_HEADS]
                pos = q0 + colk
                in_seq = jnp.logical_not(
                    jnp.logical_or(jnp.logical_and(is_first, pos < W),
                                   jnp.logical_and(is_last, pos >= n + W)))
                s = jnp.where(in_seq, s + bias_scr[idx % DA_HEADS, 0:bq, 0:bk], NEG_BIG)
                m = jnp.max(s, axis=-1, keepdims=True)
                e = jnp.exp2(s - m)
                l = jnp.sum(e, axis=-1, keepdims=True)
                probs.append(e.astype(BF16))
                inv_l.append(1.0 / l)
                lse2.append(m + jnp.log2(l))
            for it, (r, q0, slot) in enumerate(items):
                for p in range(n_pairs):
                    vp = vs_ref[p, pl.ds(start(q0, slot, DA_K_SLOT), bk), :]
                    h0 = it * DA_HEADS + 2 * p
                    o_h = [jnp.dot(probs[h0 + hh], vp, preferred_element_type=F32)
                           * inv_l[h0 + hh] for hh in range(2)]
                    rows = pl.ds(r + dil * q0, bq, stride=dil)
                    o_scr[g, p, rows, :] = jnp.where(lo_lane, o_h[0], o_h[1])
                    lse_scr[g, p, rows, :] = jnp.where(lo_lane, lse2[h0], lse2[h0 + 1])

        if n == bq:
            def class_pair(t, carry, stage=stage, attend=attend):
                stage(2 * t, 0)
                stage(2 * t + 1, 1)
                attend([(2 * t, 0, 0), (2 * t + 1, 0, 1)])
                return carry

            lax.fori_loop(0, dil // 2, class_pair, 0)
        else:
            def class_body(r, carry, stage=stage, attend=attend, bq=bq, n=n):
                stage(r, 0)

                def qb_pair(t, carry2):
                    q0 = pl.multiple_of(t * (2 * bq), 2 * bq)
                    attend([(r, q0, 0), (r, pl.multiple_of(q0 + bq, bq), 0)])
                    return carry2

                lax.fori_loop(0, n // (2 * bq), qb_pair, 0)
                return carry

            lax.fori_loop(0, dil, class_body, 0)

    rows_per_step = 256

    def merge_body(t, carry):
        t0 = pl.multiple_of(t * rows_per_step, rows_per_step)
        rows = pl.ds(t0, rows_per_step)
        for p in range(n_pairs):
            lse = [lse_scr[g, p, rows, :] for g in range(len(DA_DILATIONS))]
            mx = jnp.maximum(jnp.maximum(lse[0], lse[1]), lse[2])
            w = [jnp.exp2(v - mx) for v in lse]
            num = sum(w[g] * o_scr[g, p, rows, :] for g in range(len(DA_DILATIONS)))
            out_ref[rows, LANES * p:LANES * (p + 1)] = (num / (w[0] + w[1] + w[2])).astype(
                out_ref.dtype)
        return carry

    lax.fori_loop(0, T // rows_per_step, merge_body, 0)


def _dilated_attention(qkv):
    B, _, S, _ = qkv.shape
    T = DA_TOKENS
    n_blocks = S // T
    n_pairs = DA_WIDTH // LANES
    slopes = tuple(2.0 ** (-8.0 * (h + 1) / DA_HEADS) for h in range(DA_HEADS))

    def spec(slab, shift):
        def index(b, i):
            return (b, slab, jnp.clip(i + shift, 0, n_blocks - 1), 0)
        return pl.BlockSpec((None, n_pairs, T, LANES), index)

    win = T + 2 * DA_HALF_STEPS
    return pl.pallas_call(
        functools.partial(_dilated_kernel, n_blocks, slopes),
        grid=(B, n_blocks),
        in_specs=[spec(0, 0), spec(1, -1), spec(1, 0), spec(1, 1),
                  spec(2, -1), spec(2, 0), spec(2, 1)],
        out_specs=pl.BlockSpec((None, T, DA_WIDTH), lambda b, i: (b, i, 0)),
        out_shape=jax.ShapeDtypeStruct((B, S, DA_WIDTH), BF16),
        scratch_shapes=[
            pltpu.VMEM((n_pairs, T, LANES), BF16),
            pltpu.VMEM((n_pairs, win, LANES), BF16),
            pltpu.VMEM((n_pairs, win, LANES), BF16),
            pltpu.VMEM((len(DA_DILATIONS), n_pairs, T, LANES), F32),
            pltpu.VMEM((len(DA_DILATIONS), n_pairs, T, LANES), F32),
            pltpu.VMEM((DA_HEADS, 128, 128 + 2 * DA_HALF_STEPS), F32),
        ],
        compiler_params=_params("parallel", "parallel"),
        name="dilated_attention",
    )(qkv, qkv, qkv, qkv, qkv, qkv, qkv)


def _out_ffn_kernel(n_y, final_norm, *refs):
    x_ref = refs[0]
    y_refs = refs[1:1 + n_y]
    wo_ref, g_ref, wg_ref, wu_ref, wd_ref = refs[1 + n_y:6 + n_y]
    gf_ref = refs[6 + n_y] if final_norm else None
    out_ref = refs[-1]

    y = y_refs[0][...] if n_y == 1 else jnp.concatenate([r[...] for r in y_refs], axis=1)
    x = x_ref[...] + jnp.dot(y, wo_ref[...], preferred_element_type=F32)
    h = _rms(x, g_ref[...]).astype(BF16)
    gate = jnp.dot(h, wg_ref[...], preferred_element_type=F32)
    up = jnp.dot(h, wu_ref[...], preferred_element_type=F32)
    act = (jax.nn.silu(gate) * up).astype(BF16)
    x = x + jnp.dot(act, wd_ref[...], preferred_element_type=F32)
    if final_norm:
        x = _rms(x, gf_ref[...])
    out_ref[...] = x


def _out_ffn(x, ys, w_out, g, wg, wu, wd, g_final=None):
    B, S, D = x.shape
    rows = B * S
    tm = ROW_TILE
    x2 = x.reshape(rows, D)
    ys2 = [y.reshape(rows, y.shape[-1]) for y in ys]
    consts = [w_out, g, wg, wu, wd] + ([g_final] if g_final is not None else [])
    out = pl.pallas_call(
        functools.partial(_out_ffn_kernel, len(ys), g_final is not None),
        grid=(rows // tm,),
        in_specs=([pl.BlockSpec((tm, D), lambda i: (i, 0))]
                  + [pl.BlockSpec((tm, y.shape[-1]), lambda i: (i, 0)) for y in ys2]
                  + [pl.BlockSpec(t.shape, lambda i, nd=t.ndim: (0,) * nd,
                                  pipeline_mode=pl.Buffered(1)) for t in consts]),
        out_specs=pl.BlockSpec((tm, D), lambda i: (i, 0)),
        out_shape=jax.ShapeDtypeStruct((rows, D), F32),
        compiler_params=_params("parallel"),
        name="out_proj_ffn",
    )(x2, *ys2, *consts)
    return out.reshape(B, S, D)


def _mla_in_kernel(x_ref, g_ref, win_ref, qn_ref, kvn_ref, wq_ref, wk_ref, wvt_ref, cos_ref,
                   sin_ref, q_ref, k_ref, vt_ref):
    h = _rms(x_ref[...], g_ref[...]).astype(BF16)
    proj = jnp.dot(h, win_ref[...], preferred_element_type=F32)
    cq = _rms(proj[:, :MLA_Q_RANK], qn_ref[...]).astype(BF16)
    ckv_f32 = _rms(proj[:, MLA_Q_RANK:MLA_Q_RANK + MLA_KV_RANK], kvn_ref[...])
    ckv = ckv_f32.astype(BF16)
    base = MLA_Q_RANK + MLA_KV_RANK
    cos = cos_ref[...]
    sin = sin_ref[...]
    k_rope = proj[:, base:base + LANES] * cos + proj[:, base + LANES:base + 2 * LANES] * sin
    scale = math.log2(math.e) / math.sqrt(MLA_QK)
    for hd in range(MLA_HEADS):
        cols = slice(hd * LANES, (hd + 1) * LANES)
        qq = jnp.dot(cq, wq_ref[:, 2 * hd * LANES:2 * (hd + 1) * LANES],
                     preferred_element_type=F32)
        q_ref[:, cols] = ((qq[:, :LANES] * cos + qq[:, LANES:] * sin) * scale).astype(BF16)
    for pair in range(MLA_HEADS // 2):
        cols = slice(2 * pair * LANES, 2 * (pair + 1) * LANES)
        kn = jnp.dot(ckv, wk_ref[:, cols], preferred_element_type=F32)
        k_ref[:, cols] = (kn + jnp.concatenate([k_rope, k_rope], axis=1)).astype(BF16)
    vt = jnp.dot(wvt_ref[...], ckv_f32.T.astype(BF16), preferred_element_type=F32)
    ones = jnp.ones((MLA_ONES, vt.shape[1]), BF16)
    for hd in range(MLA_HEADS):
        vt_ref[hd * MLA_VROWS:hd * MLA_VROWS + MLA_V, :] = (
            vt[hd * MLA_V:(hd + 1) * MLA_V].astype(BF16))
        vt_ref[hd * MLA_VROWS + MLA_V:(hd + 1) * MLA_VROWS, :] = ones


def _mla_in_proj(x, g, win, qn, kvn, wq, wk, wvt, cos_tab, sin_tab):
    B, S, D = x.shape
    tm = ROW_TILE
    width = MLA_HEADS * LANES
    consts = [g, win, qn, kvn, wq, wk, wvt]
    return pl.pallas_call(
        _mla_in_kernel,
        grid=(B, S // tm),
        in_specs=([pl.BlockSpec((None, tm, D), lambda b, i: (b, i, 0))]
                  + [_const_spec(t.shape) for t in consts]
                  + [pl.BlockSpec((tm, LANES), lambda b, i: (i, 0)),
                     pl.BlockSpec((tm, LANES), lambda b, i: (i, 0))]),
        out_specs=[
            pl.BlockSpec((None, tm, width), lambda b, i: (b, i, 0)),
            pl.BlockSpec((None, tm, width), lambda b, i: (b, i, 0)),
            pl.BlockSpec((None, None, MLA_HEADS * MLA_VROWS, tm), lambda b, i: (b, i, 0, 0)),
        ],
        out_shape=[
            jax.ShapeDtypeStruct((B, S, width), BF16),
            jax.ShapeDtypeStruct((B, S, width), BF16),
            jax.ShapeDtypeStruct((B, S // tm, MLA_HEADS * MLA_VROWS, tm), BF16),
        ],
        compiler_params=_params("parallel", "parallel"),
        name="mla_in_proj",
    )(x, *consts, cos_tab, sin_tab)


def _mla_attn_kernel(q_ref, k_ref, vt_ref, o_ref, acc_scr):
    n_kb, _, bk = vt_ref.shape
    bq = q_ref.shape[0]
    qs = [q_ref[:, hh * LANES:(hh + 1) * LANES] for hh in range(2)]
    rows = vt_ref.shape[1] // 2
    nt = (((1,), (1,)), ((), ()))

    def scores(j, hh):
        return lax.dot_general(k_ref[j * bk:(j + 1) * bk, hh * LANES:(hh + 1) * LANES], qs[hh],
                               nt, preferred_element_type=F32)

    def write_output():
        outs = [acc_scr[hh, 0:MLA_V] / acc_scr[hh, MLA_V:MLA_V + 1] for hh in range(2)]
        o_ref[...] = jnp.concatenate(outs, axis=0).T.astype(o_ref.dtype)

    ref_max = [None, None]
    run_max = [None, None]
    excess = jnp.zeros((1, bq), F32)
    sts = [scores(0, hh) for hh in range(2)]
    for j in range(n_kb):
        for hh in range(2):
            blk_max = jnp.max(sts[hh], axis=0, keepdims=True)
            if j == 0:
                new_ref, alpha = blk_max, None
                run_max[hh] = blk_max
            else:
                new_ref = run_max[hh]
                alpha = jnp.exp2(ref_max[hh] - new_ref)
                excess = jnp.maximum(excess, blk_max - new_ref)
                run_max[hh] = jnp.maximum(run_max[hh], blk_max)
            ref_max[hh] = new_ref
            p = jnp.exp2((sts[hh] - new_ref).astype(BF16))
            if j + 1 < n_kb:
                sts[hh] = scores(j + 1, hh)
            pv = jnp.dot(vt_ref[j, hh * rows:(hh + 1) * rows, :], p, preferred_element_type=F32)
            acc_scr[hh] = pv if alpha is None else alpha * acc_scr[hh] + pv
    write_output()

    @pl.when(jnp.max(excess) > MLA_MAX_LAG_EXCESS)
    def _():
        acc_scr[...] = jnp.zeros_like(acc_scr)

        def body(j, ms):
            k0 = pl.multiple_of(j * bk, bk)
            new = []
            for hh in range(2):
                st = lax.dot_general(k_ref[pl.ds(k0, bk), hh * LANES:(hh + 1) * LANES], qs[hh],
                                     nt, preferred_element_type=F32)
                m_new = jnp.maximum(ms[hh], jnp.max(st, axis=0, keepdims=True))
                p = jnp.exp2(st - m_new).astype(BF16)
                pv = jnp.dot(vt_ref[j, hh * rows:(hh + 1) * rows, :], p,
                             preferred_element_type=F32)
                acc_scr[hh] = jnp.exp2(ms[hh] - m_new) * acc_scr[hh] + pv
                new.append(m_new)
            return tuple(new)

        lax.fori_loop(0, n_kb, body, tuple(jnp.full((1, bq), NEG_BIG, F32) for _ in range(2)))
        write_output()


def _mla_attention(q, k, vt):
    B, S, _ = q.shape
    n_kb = vt.shape[1]
    bq = MLA_Q_TILE
    n_pairs = MLA_HEADS // 2
    return pl.pallas_call(
        _mla_attn_kernel,
        grid=(B, n_pairs, S // bq),
        in_specs=[
            pl.BlockSpec((None, bq, 2 * LANES), lambda b, p, i: (b, i, p)),
            pl.BlockSpec((None, S, 2 * LANES), lambda b, p, i: (b, 0, p)),
            pl.BlockSpec((None, n_kb, 2 * MLA_VROWS, MLA_K_TILE), lambda b, p, i: (b, 0, p, 0)),
        ],
        out_specs=pl.BlockSpec((None, bq, 2 * MLA_V), lambda b, p, i: (b, i, p)),
        out_shape=jax.ShapeDtypeStruct((B, S, MLA_HEADS * MLA_V), BF16),
        scratch_shapes=[
            pltpu.VMEM((2, MLA_VROWS, bq), F32),
        ],
        compiler_params=_params("parallel", "parallel", "arbitrary"),
        name="mla_attention",
    )(q, k, vt)


def _block_diag(w):
    nb, d, _ = w.shape
    eye = jnp.eye(nb, dtype=w.dtype)
    return (eye[:, None, :, None] * w[:, :, None, :]).reshape(nb * d, nb * d)


def _rot_cols(w):
    half = MLA_ROPE // 2
    return jnp.concatenate([-w[..., half:], w[..., :half]], axis=-1)


def _pad_lanes(w, offset):
    return jnp.pad(w, [(0, 0)] * (w.ndim - 1) + [(offset, LANES - offset - w.shape[-1])])


def _prepare(norm_mix, norm_ffn, norm_final, ab_w_in, ab_conv_w, ab_conv_b, rg_w_a, rg_b_a,
             rg_w_i, rg_b_i, rg_lam, ab_w_out, mla_w_in, mla_q_norm, mla_w_qb, mla_kv_norm,
             mla_w_kvb, mla_w_out, ffn_w_gate, ffn_w_up, ffn_w_down):
    row = lambda v: v.reshape(1, -1).astype(F32)
    P = {}
    P["norm_mix"] = [row(norm_mix[l]) for l in range(2)]
    P["norm_ffn"] = [row(norm_ffn[l]) for l in range(2)]
    P["norm_final"] = row(norm_final)
    P["ab_w_in"] = ab_w_in[0].astype(BF16)
    P["conv_w"] = ab_conv_w[0].astype(F32)
    P["conv_b"] = row(ab_conv_b[0])
    P["rg_w"] = [jnp.concatenate([_block_diag(rg_w_a[0, d]), _block_diag(rg_w_i[0, d])],
                                 axis=1).astype(BF16) for d in range(2)]
    P["rg_b"] = [jnp.concatenate([rg_b_a[0, d], rg_b_i[0, d]]).reshape(1, -1) for d in range(2)]
    P["rg_lam"] = [row(rg_lam[0, d]) for d in range(2)]
    P["ab_w_out"] = ab_w_out[0].astype(BF16)

    w_in = mla_w_in[0]
    base = MLA_Q_RANK + MLA_KV_RANK
    w_kr = w_in[:, base:]
    P["mla_w_in"] = jnp.concatenate(
        [w_in[:, :base], _pad_lanes(w_kr, MLA_NOPE), _pad_lanes(_rot_cols(w_kr), MLA_NOPE)],
        axis=1).astype(BF16)
    P["mla_q_norm"] = row(mla_q_norm[0])
    P["mla_kv_norm"] = row(mla_kv_norm[0])
    wq = mla_w_qb[0].reshape(MLA_Q_RANK, MLA_HEADS, MLA_QK)
    wq_plain = _pad_lanes(wq, 0)
    wq_rot = _pad_lanes(_rot_cols(wq[..., MLA_NOPE:]), MLA_NOPE)
    P["mla_wq"] = jnp.concatenate([wq_plain, wq_rot], axis=-1).reshape(MLA_Q_RANK, -1).astype(BF16)
    wkv = mla_w_kvb[0].reshape(MLA_KV_RANK, MLA_HEADS, MLA_NOPE + MLA_V)
    P["mla_wk"] = _pad_lanes(wkv[..., :MLA_NOPE], 0).reshape(MLA_KV_RANK, -1).astype(BF16)
    P["mla_wvt"] = wkv[..., MLA_NOPE:].reshape(MLA_KV_RANK, -1).T.astype(BF16)
    P["mla_w_out"] = mla_w_out[0].astype(BF16)
    P["ffn"] = [(ffn_w_gate[l].astype(BF16), ffn_w_up[l].astype(BF16),
                 ffn_w_down[l].astype(BF16)) for l in range(2)]
    return P


def _rope_slabs(S):
    inv_freq = 1.0 / (ROPE_THETA ** (jnp.arange(0, MLA_ROPE, 2, dtype=F32) / MLA_ROPE))
    ang = jnp.arange(S, dtype=F32)[:, None] * inv_freq[None, :]
    cos, sin = jnp.cos(ang), jnp.sin(ang)
    pad = LANES - MLA_QK
    cos_tab = jnp.concatenate([jnp.ones((S, MLA_NOPE), F32), cos, cos, jnp.zeros((S, pad), F32)], 1)
    sin_tab = jnp.concatenate([jnp.zeros((S, MLA_NOPE), F32), sin, sin, jnp.zeros((S, pad), F32)], 1)
    return cos_tab, sin_tab


def _trunk(x, P):
    S = x.shape[1]
    xg, qkv = _ab_in_proj(x, P["norm_mix"][0], P["ab_w_in"])
    hf = _rglru_pass(xg, None, P["conv_w"], P["conv_b"], P["rg_w"][0], P["rg_b"][0],
                     P["rg_lam"][0], reverse=False)
    y_rnn = _rglru_pass(xg, hf, P["conv_w"], P["conv_b"], P["rg_w"][1], P["rg_b"][1],
                        P["rg_lam"][1], reverse=True)
    o = _dilated_attention(qkv)
    x = _out_ffn(x, [y_rnn, o], P["ab_w_out"], P["norm_ffn"][0], *P["ffn"][0])
    cos_tab, sin_tab = _rope_slabs(S)
    q, k, vt = _mla_in_proj(x, P["norm_mix"][1], P["mla_w_in"], P["mla_q_norm"], P["mla_kv_norm"],
                            P["mla_wq"], P["mla_wk"], P["mla_wvt"], cos_tab, sin_tab)
    o = _mla_attention(q, k, vt)
    return _out_ffn(x, [o], P["mla_w_out"], P["norm_ffn"][1], *P["ffn"][1],
                    g_final=P["norm_final"])


def kernel(x_prompt, x_sample, norm_mix, norm_ffn, norm_final, ab_w_in, ab_conv_w, ab_conv_b, rg_w_a, rg_b_a, rg_w_i, rg_b_i, rg_lam, ab_w_out, mla_w_in, mla_q_norm, mla_w_qb, mla_kv_norm, mla_w_kvb, mla_w_out, ffn_w_gate, ffn_w_up, ffn_w_down):
    P = _prepare(norm_mix, norm_ffn, norm_final, ab_w_in, ab_conv_w, ab_conv_b, rg_w_a, rg_b_a,
                 rg_w_i, rg_b_i, rg_lam, ab_w_out, mla_w_in, mla_q_norm, mla_w_qb, mla_kv_norm,
                 mla_w_kvb, mla_w_out, ffn_w_gate, ffn_w_up, ffn_w_down)
    return (_trunk(x_prompt, P), _trunk(x_sample, P))
```

```python
import functools
import math

import jax
import jax.numpy as jnp
from jax import lax
from jax.experimental import pallas as pl
from jax.experimental.pallas import tpu as pltpu

F32 = jnp.float32
BF16 = jnp.bfloat16

D_MODEL = 1024
EPS = 1e-6
RG_WIDTH = 512
RG_BLOCKS = 8
RG_C = 8.0
DA_HEADS = 8
DA_HEAD_DIM = 64
DA_WIDTH = 512
DA_DILATIONS = (1, 4, 16)
DA_HALF_STEPS = 64
DA_TOKENS = DA_HALF_STEPS * max(DA_DILATIONS)
DA_Q_SLOT = 128
DA_K_SLOT = 256
MLA_HEADS = 16
MLA_Q_RANK = 384
MLA_KV_RANK = 256
MLA_NOPE = 64
MLA_ROPE = 32
MLA_V = 64
MLA_QK = MLA_NOPE + MLA_ROPE
MLA_ONES = 16
MLA_VROWS = MLA_V + MLA_ONES
MLA_MAX_LAG_EXCESS = 100.0
ROPE_THETA = 10000.0
FFN_HIDDEN = 2816
NEG_BIG = -1e30
LANES = 128

VMEM_LIMIT = 56 * 1024 * 1024

ROW_TILE = 512
SCAN_TILE = 512
SCAN_PAD = 8
MLA_Q_TILE = 2048
MLA_K_TILE = ROW_TILE


def _params(*sem):
    return pltpu.CompilerParams(dimension_semantics=sem, vmem_limit_bytes=VMEM_LIMIT)


def _const_spec(shape):
    nd = len(shape)
    return pl.BlockSpec(shape, lambda *_: (0,) * nd)


def _rms(x, g):
    return x * lax.rsqrt(jnp.mean(x * x, axis=-1, keepdims=True) + EPS) * g


def _ab_in_kernel(x_ref, g_ref, w_ref, xg_ref, qkv_ref):
    h = _rms(x_ref[...], g_ref[...]).astype(BF16)
    xg_ref[...] = jnp.dot(h, w_ref[:, :2 * RG_WIDTH], preferred_element_type=F32)
    n_slabs = 3 * DA_WIDTH // LANES
    qkv = jnp.dot(h, w_ref[:, 2 * RG_WIDTH:], preferred_element_type=F32)
    for j in range(n_slabs):
        qkv_ref[j] = qkv[:, LANES * j:LANES * (j + 1)]


def _ab_in_proj(x, g, w):
    B, S, D = x.shape
    n_slabs = 3 * DA_WIDTH // LANES
    tm = ROW_TILE
    return pl.pallas_call(
        _ab_in_kernel,
        grid=(B, S // tm),
        in_specs=[
            pl.BlockSpec((None, tm, D), lambda b, i: (b, i, 0)),
            _const_spec(g.shape),
            _const_spec(w.shape),
        ],
        out_specs=[
            pl.BlockSpec((None, tm, 2 * RG_WIDTH), lambda b, i: (b, i, 0)),
            pl.BlockSpec((None, n_slabs, tm, LANES), lambda b, i: (b, 0, i, 0)),
        ],
        out_shape=[
            jax.ShapeDtypeStruct((B, S, 2 * RG_WIDTH), F32),
            jax.ShapeDtypeStruct((B, n_slabs, S, LANES), F32),
        ],
        compiler_params=_params("parallel", "parallel"),
        name="ab_in_proj",
    )(x, g, w)


def _shift_rows(x, d, edge, toward_end):
    t = SCAN_PAD
    n = x.shape[0] // t
    sub = lax.broadcasted_iota(jnp.int32, (t, x.shape[1]), 0)
    tiles = [x[t * i:t * (i + 1)] for i in range(n)]
    if toward_end:
        rot = [pltpu.roll(b, d, 0) for b in [edge] + tiles]
        out = [jnp.where(sub < d, rot[i], rot[i + 1]) for i in range(n)]
    else:
        rot = [pltpu.roll(b, t - d, 0) for b in tiles + [edge]]
        out = [jnp.where(sub >= t - d, rot[i + 1], rot[i]) for i in range(n)]
    return jnp.concatenate(out, axis=0)


def _rglru_kernel(reverse, n_chunks, *refs):
    if reverse:
        (x_ref, xp_ref, xn_ref, gate_ref, hf_ref, cw_ref, cb_ref, wg_ref, bg_ref, lam_ref,
         out_ref, carry_ref) = refs
    else:
        (x_ref, xp_ref, xn_ref, cw_ref, cb_ref, wg_ref, bg_ref, lam_ref,
         out_ref, carry_ref) = refs
    i = pl.program_id(1)
    c = (n_chunks - 1 - i) if reverse else i
    tc, width = x_ref.shape

    @pl.when(i == 0)
    def _():
        carry_ref[...] = jnp.zeros_like(carry_ref)

    x = x_ref[...]
    prev = jnp.where(c > 0, xp_ref[...], 0.0)
    nxt = jnp.where(c < n_chunks - 1, xn_ref[...], 0.0)
    cw = cw_ref[...]
    xc = (cw[0:1] * _shift_rows(x, 2, prev, True) + cw[1:2] * _shift_rows(x, 1, prev, True)
          + cw[2:3] * x + cw[3:4] * _shift_rows(x, 1, nxt, False) + cb_ref[...])

    z = jnp.dot(xc.astype(BF16), wg_ref[...], preferred_element_type=F32) + bg_ref[...]
    gates = 0.5 * jnp.tanh(0.5 * z) + 0.5
    r = gates[:, :width]
    gi = gates[:, width:]
    log_a = (-RG_C) * r * jax.nn.softplus(-lam_ref[...])
    a = jnp.exp(log_a)
    z2 = -jnp.tanh(log_a) * (a * a + 1.0)
    u = jnp.where(z2 > 0.0, z2 * lax.rsqrt(z2), 0.0) * (gi * xc)

    one_edge = jnp.ones((SCAN_PAD, width), F32)
    zero_edge = jnp.zeros((SCAN_PAD, width), F32)
    d = 1
    while d < tc:
        if d < SCAN_PAD:
            a_sh = _shift_rows(a, d, one_edge, not reverse)
            u = a * _shift_rows(u, d, zero_edge, not reverse) + u
            a = a * a_sh
        elif reverse:
            u = jnp.concatenate([a[:tc - d] * u[d:] + u[:tc - d], u[tc - d:]], axis=0)
            a = jnp.concatenate([a[:tc - d] * a[d:], a[tc - d:]], axis=0)
        else:
            u = jnp.concatenate([u[:d], a[d:] * u[:tc - d] + u[d:]], axis=0)
            a = jnp.concatenate([a[:d], a[d:] * a[:tc - d]], axis=0)
        d *= 2
    h = u + a * carry_ref[...]
    carry_ref[...] = h[0:1] if reverse else h[tc - 1:tc]

    if reverse:
        out_ref[...] = (jax.nn.gelu(gate_ref[...]) * (hf_ref[...] + h)).astype(out_ref.dtype)
    else:
        out_ref[...] = h


def _rglru_pass(xg, hf, conv_w, conv_b, wg, bg, lam, reverse):
    B, S, _ = xg.shape
    tc = SCAN_TILE
    n_chunks = S // tc
    sub = SCAN_PAD

    def chunk(i):
        return (n_chunks - 1 - i) if reverse else i

    x_spec = pl.BlockSpec((None, tc, RG_WIDTH), lambda b, i: (b, chunk(i), 0))
    prev_spec = pl.BlockSpec(
        (None, sub, RG_WIDTH), lambda b, i: (b, jnp.maximum(chunk(i) * (tc // sub) - 1, 0), 0))
    next_spec = pl.BlockSpec(
        (None, sub, RG_WIDTH),
        lambda b, i: (b, jnp.minimum((chunk(i) + 1) * (tc // sub), S // sub - 1), 0))
    consts = [conv_w, conv_b, wg, bg, lam]
    const_specs = [_const_spec(t.shape) for t in consts]
    if reverse:
        gate_spec = pl.BlockSpec((None, tc, RG_WIDTH), lambda b, i: (b, chunk(i), 1))
        hf_spec = pl.BlockSpec((None, tc, RG_WIDTH), lambda b, i: (b, chunk(i), 0))
        inputs = [xg, xg, xg, xg, hf] + consts
        in_specs = [x_spec, prev_spec, next_spec, gate_spec, hf_spec] + const_specs
        out_dtype = BF16
    else:
        inputs = [xg, xg, xg] + consts
        in_specs = [x_spec, prev_spec, next_spec] + const_specs
        out_dtype = F32
    return pl.pallas_call(
        functools.partial(_rglru_kernel, reverse, n_chunks),
        grid=(B, n_chunks),
        in_specs=in_specs,
        out_specs=pl.BlockSpec((None, tc, RG_WIDTH), lambda b, i: (b, chunk(i), 0)),
        out_shape=jax.ShapeDtypeStruct((B, S, RG_WIDTH), out_dtype),
        scratch_shapes=[pltpu.VMEM((1, RG_WIDTH), F32)],
        compiler_params=_params("parallel", "arbitrary"),
        name="rglru_bwd" if reverse else "rglru_fwd",
    )(*inputs)


def _dilated_kernel(n_blocks, slopes, q_ref, kp_ref, kc_ref, kn_ref, vp_ref, vc_ref, vn_ref,
                    out_ref, qs_ref, ks_ref, vs_ref, o_scr, lse_scr, bias_scr):
    i = pl.program_id(1)
    T = DA_TOKENS
    W = DA_HALF_STEPS
    n_pairs = DA_WIDTH // LANES
    log2e = math.log2(math.e)
    scale = log2e / math.sqrt(DA_HEAD_DIM)
    is_first = i == 0
    is_last = i == n_blocks - 1

    for g, dil in enumerate(DA_DILATIONS):
        n = T // dil
        bq = min(n, 128)
        bk = bq + 2 * W
        rowq = lax.broadcasted_iota(jnp.int32, (bq, bk), 0)
        colk = lax.broadcasted_iota(jnp.int32, (bq, bk), 1)
        rel = colk - W - rowq
        band = jnp.abs(rel) <= W
        neg_dist = -(jnp.abs(rel) * dil).astype(F32)
        lane = lax.broadcasted_iota(jnp.int32, (bq, LANES), 1)
        lo_lane = lane < DA_HEAD_DIM
        for head in range(DA_HEADS):
            bias_scr[head, 0:bq, 0:bk] = jnp.where(band, (slopes[head] * log2e) * neg_dist,
                                                   NEG_BIG)

        def stage(r, slot, dil=dil, n=n):
            qo, ko = slot * DA_Q_SLOT, slot * DA_K_SLOT
            for p in range(n_pairs):
                qs_ref[p, qo:qo + n] = (q_ref[p, pl.ds(r, n, stride=dil), :] * scale).astype(BF16)
                for src_p, src_c, src_n, dst in ((kp_ref, kc_ref, kn_ref, ks_ref),
                                                 (vp_ref, vc_ref, vn_ref, vs_ref)):
                    dst[p, ko:ko + W] = src_p[p, pl.ds(r + dil * (n - W), W, stride=dil),
                                              :].astype(BF16)
                    dst[p, ko + W:ko + W + n] = src_c[p, pl.ds(r, n, stride=dil), :].astype(BF16)
                    dst[p, ko + W + n:ko + 2 * W + n] = src_n[p, pl.ds(r, W, stride=dil),
                                                             :].astype(BF16)

        def attend(items, dil=dil, n=n, bq=bq, bk=bk, g=g, colk=colk, lo_lane=lo_lane):
            def start(q0, slot, pitch):
                return q0 if slot == 0 else slot * pitch + q0

            scores = []
            for r, q0, slot in items:
                for p in range(n_pairs):
                    qp = qs_ref[p, pl.ds(start(q0, slot, DA_Q_SLOT), bq), :]
                    kp = ks_ref[p, pl.ds(start(q0, slot, DA_K_SLOT), bk), :]
                    for hh in range(2):
                        qh = jnp.where(lo_lane if hh == 0 else jnp.logical_not(lo_lane), qp,
                                       jnp.zeros_like(qp))
                        scores.append(lax.dot_general(qh, kp, (((1,), (1,)), ((), ())),
                                                      preferred_element_type=F32))
            probs, lse2, inv_l = [], [], []
            for idx, s in enumerate(scores):
                r, q0, slot = items[idx // DA_HEADS]
                pos = q0 + colk
                in_seq = jnp.logical_not(
                    jnp.logical_or(jnp.logical_and(is_first, pos < W),
                                   jnp.logical_and(is_last, pos >= n + W)))
                s = jnp.where(in_seq, s + bias_scr[idx % DA_HEADS, 0:bq, 0:bk], NEG_BIG)
                m = jnp.max(s, axis=-1, keepdims=True)
                e = jnp.exp2(s - m)
                l = jnp.sum(e, axis=-1, keepdims=True)
                probs.append(e.astype(BF16))
                inv_l.append(1.0 / l)
                lse2.append(m + jnp.log2(l))
            for it, (r, q0, slot) in enumerate(items):
                for p in range(n_pairs):
                    vp = vs_ref[p, pl.ds(start(q0, slot, DA_K_SLOT), bk), :]
                    h0 = it * DA_HEADS + 2 * p
                    o_h = [jnp.dot(probs[h0 + hh], vp, preferred_element_type=F32)
                           * inv_l[h0 + hh] for hh in range(2)]
                    rows = pl.ds(r + dil * q0, bq, stride=dil)
                    o_scr[g, p, rows, :] = jnp.where(lo_lane, o_h[0], o_h[1])
                    lse_scr[g, p, rows, :] = jnp.where(lo_lane, lse2[h0], lse2[h0 + 1])

        if n == bq:
            def class_pair(t, carry, stage=stage, attend=attend):
                stage(2 * t, 0)
                stage(2 * t + 1, 1)
                attend([(2 * t, 0, 0), (2 * t + 1, 0, 1)])
                return carry

            lax.fori_loop(0, dil // 2, class_pair, 0)
        else:
            def class_body(r, carry, stage=stage, attend=attend, bq=bq, n=n):
                stage(r, 0)

                def qb_pair(t, carry2):
                    q0 = pl.multiple_of(t * (2 * bq), 2 * bq)
                    attend([(r, q0, 0), (r, pl.multiple_of(q0 + bq, bq), 0)])
                    return carry2

                lax.fori_loop(0, n // (2 * bq), qb_pair, 0)
                return carry

            lax.fori_loop(0, dil, class_body, 0)

    rows_per_step = 256

    def merge_body(t, carry):
        t0 = pl.multiple_of(t * rows_per_step, rows_per_step)
        rows = pl.ds(t0, rows_per_step)
        for p in range(n_pairs):
            lse = [lse_scr[g, p, rows, :] for g in range(len(DA_DILATIONS))]
            mx = jnp.maximum(jnp.maximum(lse[0], lse[1]), lse[2])
            w = [jnp.exp2(v - mx) for v in lse]
            num = sum(w[g] * o_scr[g, p, rows, :] for g in range(len(DA_DILATIONS)))
            out_ref[rows, LANES * p:LANES * (p + 1)] = (num / (w[0] + w[1] + w[2])).astype(
                out_ref.dtype)
        return carry

    lax.fori_loop(0, T // rows_per_step, merge_body, 0)


def _dilated_attention(qkv):
    B, _, S, _ = qkv.shape
    T = DA_TOKENS
    n_blocks = S // T
    n_pairs = DA_WIDTH // LANES
    slopes = tuple(2.0 ** (-8.0 * (h + 1) / DA_HEADS) for h in range(DA_HEADS))

    def spec(slab, shift):
        def index(b, i):
            return (b, slab, jnp.clip(i + shift, 0, n_blocks - 1), 0)
        return pl.BlockSpec((None, n_pairs, T, LANES), index)

    win = T + 2 * DA_HALF_STEPS
    return pl.pallas_call(
        functools.partial(_dilated_kernel, n_blocks, slopes),
        grid=(B, n_blocks),
        in_specs=[spec(0, 0), spec(1, -1), spec(1, 0), spec(1, 1),
                  spec(2, -1), spec(2, 0), spec(2, 1)],
        out_specs=pl.BlockSpec((None, T, DA_WIDTH), lambda b, i: (b, i, 0)),
        out_shape=jax.ShapeDtypeStruct((B, S, DA_WIDTH), BF16),
        scratch_shapes=[
            pltpu.VMEM((n_pairs, T, LANES), BF16),
            pltpu.VMEM((n_pairs, win, LANES), BF16),
            pltpu.VMEM((n_pairs, win, LANES), BF16),
            pltpu.VMEM((len(DA_DILATIONS), n_pairs, T, LANES), F32),
            pltpu.VMEM((len(DA_DILATIONS), n_pairs, T, LANES), F32),
            pltpu.VMEM((DA_HEADS, 128, 128 + 2 * DA_HALF_STEPS), F32),
        ],
        compiler_params=_params("parallel", "parallel"),
        name="dilated_attention",
    )(qkv, qkv, qkv, qkv, qkv, qkv, qkv)


def _out_ffn_kernel(n_y, final_norm, *refs):
    x_ref = refs[0]
    y_refs = refs[1:1 + n_y]
    wo_ref, g_ref, wg_ref, wu_ref, wd_ref = refs[1 + n_y:6 + n_y]
    gf_ref = refs[6 + n_y] if final_norm else None
    out_ref = refs[-1]

    y = y_refs[0][...] if n_y == 1 else jnp.concatenate([r[...] for r in y_refs], axis=1)
    x = x_ref[...] + jnp.dot(y, wo_ref[...], preferred_element_type=F32)
    h = _rms(x, g_ref[...]).astype(BF16)
    gate = jnp.dot(h, wg_ref[...], preferred_element_type=F32)
    up = jnp.dot(h, wu_ref[...], preferred_element_type=F32)
    act = (jax.nn.silu(gate) * up).astype(BF16)
    x = x + jnp.dot(act, wd_ref[...], preferred_element_type=F32)
    if final_norm:
        x = _rms(x, gf_ref[...])
    out_ref[...] = x


def _out_ffn(x, ys, w_out, g, wg, wu, wd, g_final=None):
    B, S, D = x.shape
    rows = B * S
    tm = ROW_TILE
    x2 = x.reshape(rows, D)
    ys2 = [y.reshape(rows, y.shape[-1]) for y in ys]
    consts = [w_out, g, wg, wu, wd] + ([g_final] if g_final is not None else [])
    out = pl.pallas_call(
        functools.partial(_out_ffn_kernel, len(ys), g_final is not None),
        grid=(rows // tm,),
        in_specs=([pl.BlockSpec((tm, D), lambda i: (i, 0))]
                  + [pl.BlockSpec((tm, y.shape[-1]), lambda i: (i, 0)) for y in ys2]
                  + [pl.BlockSpec(t.shape, lambda i, nd=t.ndim: (0,) * nd,
                                  pipeline_mode=pl.Buffered(1)) for t in consts]),
        out_specs=pl.BlockSpec((tm, D), lambda i: (i, 0)),
        out_shape=jax.ShapeDtypeStruct((rows, D), F32),
        compiler_params=_params("parallel"),
        name="out_proj_ffn",
    )(x2, *ys2, *consts)
    return out.reshape(B, S, D)


def _mla_in_kernel(x_ref, g_ref, win_ref, qn_ref, kvn_ref, wq_ref, wk_ref, wvt_ref, cos_ref,
                   sin_ref, q_ref, k_ref, vt_ref):
    h = _rms(x_ref[...], g_ref[...]).astype(BF16)
    proj = jnp.dot(h, win_ref[...], preferred_element_type=F32)
    cq = _rms(proj[:, :MLA_Q_RANK], qn_ref[...]).astype(BF16)
    ckv_f32 = _rms(proj[:, MLA_Q_RANK:MLA_Q_RANK + MLA_KV_RANK], kvn_ref[...])
    ckv = ckv_f32.astype(BF16)
    base = MLA_Q_RANK + MLA_KV_RANK
    cos = cos_ref[...]
    sin = sin_ref[...]
    k_rope = proj[:, base:base + LANES] * cos + proj[:, base + LANES:base + 2 * LANES] * sin
    scale = math.log2(math.e) / math.sqrt(MLA_QK)
    for hd in range(MLA_HEADS):
        cols = slice(hd * LANES, (hd + 1) * LANES)
        qq = jnp.dot(cq, wq_ref[:, 2 * hd * LANES:2 * (hd + 1) * LANES],
                     preferred_element_type=F32)
        q_ref[:, cols] = ((qq[:, :LANES] * cos + qq[:, LANES:] * sin) * scale).astype(BF16)
    for pair in range(MLA_HEADS // 2):
        cols = slice(2 * pair * LANES, 2 * (pair + 1) * LANES)
        kn = jnp.dot(ckv, wk_ref[:, cols], preferred_element_type=F32)
        k_ref[:, cols] = (kn + jnp.concatenate([k_rope, k_rope], axis=1)).astype(BF16)
    vt = jnp.dot(wvt_ref[...], ckv_f32.T.astype(BF16), preferred_element_type=F32)
    ones = jnp.ones((MLA_ONES, vt.shape[1]), BF16)
    for hd in range(MLA_HEADS):
        vt_ref[hd * MLA_VROWS:hd * MLA_VROWS + MLA_V, :] = (
            vt[hd * MLA_V:(hd + 1) * MLA_V].astype(BF16))
        vt_ref[hd * MLA_VROWS + MLA_V:(hd + 1) * MLA_VROWS, :] = ones


def _mla_in_proj(x, g, win, qn, kvn, wq, wk, wvt, cos_tab, sin_tab):
    B, S, D = x.shape
    tm = ROW_TILE
    width = MLA_HEADS * LANES
    consts = [g, win, qn, kvn, wq, wk, wvt]
    return pl.pallas_call(
        _mla_in_kernel,
        grid=(B, S // tm),
        in_specs=([pl.BlockSpec((None, tm, D), lambda b, i: (b, i, 0))]
                  + [_const_spec(t.shape) for t in consts]
                  + [pl.BlockSpec((tm, LANES), lambda b, i: (i, 0)),
                     pl.BlockSpec((tm, LANES), lambda b, i: (i, 0))]),
        out_specs=[
            pl.BlockSpec((None, tm, width), lambda b, i: (b, i, 0)),
            pl.BlockSpec((None, tm, width), lambda b, i: (b, i, 0)),
            pl.BlockSpec((None, None, MLA_HEADS * MLA_VROWS, tm), lambda b, i: (b, i, 0, 0)),
        ],
        out_shape=[
            jax.ShapeDtypeStruct((B, S, width), BF16),
            jax.ShapeDtypeStruct((B, S, width), BF16),
            jax.ShapeDtypeStruct((B, S // tm, MLA_HEADS * MLA_VROWS, tm), BF16),
        ],
        compiler_params=_params("parallel", "parallel"),
        name="mla_in_proj",
    )(x, *consts, cos_tab, sin_tab)


def _mla_attn_kernel(q_ref, k_ref, vt_ref, o_ref, acc_scr):
    n_kb, _, bk = vt_ref.shape
    bq = q_ref.shape[0]
    qs = [q_ref[:, hh * LANES:(hh + 1) * LANES] for hh in range(2)]
    rows = vt_ref.shape[1] // 2
    nt = (((1,), (1,)), ((), ()))

    def scores(j, hh):
        return lax.dot_general(k_ref[j * bk:(j + 1) * bk, hh * LANES:(hh + 1) * LANES], qs[hh],
                               nt, preferred_element_type=F32)

    def write_output():
        outs = [acc_scr[hh, 0:MLA_V] / acc_scr[hh, MLA_V:MLA_V + 1] for hh in range(2)]
        o_ref[...] = jnp.concatenate(outs, axis=0).T.astype(o_ref.dtype)

    ref_max = [None, None]
    run_max = [None, None]
    excess = jnp.zeros((1, bq), F32)
    sts = [scores(0, hh) for hh in range(2)]
    for j in range(n_kb):
        for hh in range(2):
            blk_max = jnp.max(sts[hh], axis=0, keepdims=True)
            if j == 0:
                new_ref, alpha = blk_max, None
                run_max[hh] = blk_max
            else:
                new_ref = run_max[hh]
                alpha = jnp.exp2(ref_max[hh] - new_ref)
                excess = jnp.maximum(excess, blk_max - new_ref)
                run_max[hh] = jnp.maximum(run_max[hh], blk_max)
            ref_max[hh] = new_ref
            p = jnp.exp2(sts[hh] - new_ref).astype(BF16)
            if j + 1 < n_kb:
                sts[hh] = scores(j + 1, hh)
            pv = jnp.dot(vt_ref[j, hh * rows:(hh + 1) * rows, :], p, preferred_element_type=F32)
            acc_scr[hh] = pv if alpha is None else alpha * acc_scr[hh] + pv
    write_output()

    @pl.when(jnp.max(excess) > MLA_MAX_LAG_EXCESS)
    def _():
        acc_scr[...] = jnp.zeros_like(acc_scr)

        def body(j, ms):
            k0 = pl.multiple_of(j * bk, bk)
            new = []
            for hh in range(2):
                st = lax.dot_general(k_ref[pl.ds(k0, bk), hh * LANES:(hh + 1) * LANES], qs[hh],
                                     nt, preferred_element_type=F32)
                m_new = jnp.maximum(ms[hh], jnp.max(st, axis=0, keepdims=True))
                p = jnp.exp2(st - m_new).astype(BF16)
                pv = jnp.dot(vt_ref[j, hh * rows:(hh + 1) * rows, :], p,
                             preferred_element_type=F32)
                acc_scr[hh] = jnp.exp2(ms[hh] - m_new) * acc_scr[hh] + pv
                new.append(m_new)
            return tuple(new)

        lax.fori_loop(0, n_kb, body, tuple(jnp.full((1, bq), NEG_BIG, F32) for _ in range(2)))
        write_output()


def _mla_attention(q, k, vt):
    B, S, _ = q.shape
    n_kb = vt.shape[1]
    bq = MLA_Q_TILE
    n_pairs = MLA_HEADS // 2
    return pl.pallas_call(
        _mla_attn_kernel,
        grid=(B, n_pairs, S // bq),
        in_specs=[
            pl.BlockSpec((None, bq, 2 * LANES), lambda b, p, i: (b, i, p)),
            pl.BlockSpec((None, S, 2 * LANES), lambda b, p, i: (b, 0, p)),
            pl.BlockSpec((None, n_kb, 2 * MLA_VROWS, MLA_K_TILE), lambda b, p, i: (b, 0, p, 0)),
        ],
        out_specs=pl.BlockSpec((None, bq, 2 * MLA_V), lambda b, p, i: (b, i, p)),
        out_shape=jax.ShapeDtypeStruct((B, S, MLA_HEADS * MLA_V), BF16),
        scratch_shapes=[
            pltpu.VMEM((2, MLA_VROWS, bq), F32),
        ],
        compiler_params=_params("parallel", "parallel", "arbitrary"),
        name="mla_attention",
    )(q, k, vt)


def _block_diag(w):
    nb, d, _ = w.shape
    eye = jnp.eye(nb, dtype=w.dtype)
    return (eye[:, None, :, None] * w[:, :, None, :]).reshape(nb * d, nb * d)


def _rot_cols(w):
    half = MLA_ROPE // 2
    return jnp.concatenate([-w[..., half:], w[..., :half]], axis=-1)


def _pad_lanes(w, offset):
    return jnp.pad(w, [(0, 0)] * (w.ndim - 1) + [(offset, LANES - offset - w.shape[-1])])


def _prepare(norm_mix, norm_ffn, norm_final, ab_w_in, ab_conv_w, ab_conv_b, rg_w_a, rg_b_a,
             rg_w_i, rg_b_i, rg_lam, ab_w_out, mla_w_in, mla_q_norm, mla_w_qb, mla_kv_norm,
             mla_w_kvb, mla_w_out, ffn_w_gate, ffn_w_up, ffn_w_down):
    row = lambda v: v.reshape(1, -1).astype(F32)
    P = {}
    P["norm_mix"] = [row(norm_mix[l]) for l in range(2)]
    P["norm_ffn"] = [row(norm_ffn[l]) for l in range(2)]
    P["norm_final"] = row(norm_final)
    P["ab_w_in"] = ab_w_in[0].astype(BF16)
    P["conv_w"] = ab_conv_w[0].astype(F32)
    P["conv_b"] = row(ab_conv_b[0])
    P["rg_w"] = [jnp.concatenate([_block_diag(rg_w_a[0, d]), _block_diag(rg_w_i[0, d])],
                                 axis=1).astype(BF16) for d in range(2)]
    P["rg_b"] = [jnp.concatenate([rg_b_a[0, d], rg_b_i[0, d]]).reshape(1, -1) for d in range(2)]
    P["rg_lam"] = [row(rg_lam[0, d]) for d in range(2)]
    P["ab_w_out"] = ab_w_out[0].astype(BF16)

    w_in = mla_w_in[0]
    base = MLA_Q_RANK + MLA_KV_RANK
    w_kr = w_in[:, base:]
    P["mla_w_in"] = jnp.concatenate(
        [w_in[:, :base], _pad_lanes(w_kr, MLA_NOPE), _pad_lanes(_rot_cols(w_kr), MLA_NOPE)],
        axis=1).astype(BF16)
    P["mla_q_norm"] = row(mla_q_norm[0])
    P["mla_kv_norm"] = row(mla_kv_norm[0])
    wq = mla_w_qb[0].reshape(MLA_Q_RANK, MLA_HEADS, MLA_QK)
    wq_plain = _pad_lanes(wq, 0)
    wq_rot = _pad_lanes(_rot_cols(wq[..., MLA_NOPE:]), MLA_NOPE)
    P["mla_wq"] = jnp.concatenate([wq_plain, wq_rot], axis=-1).reshape(MLA_Q_RANK, -1).astype(BF16)
    wkv = mla_w_kvb[0].reshape(MLA_KV_RANK, MLA_HEADS, MLA_NOPE + MLA_V)
    P["mla_wk"] = _pad_lanes(wkv[..., :MLA_NOPE], 0).reshape(MLA_KV_RANK, -1).astype(BF16)
    P["mla_wvt"] = wkv[..., MLA_NOPE:].reshape(MLA_KV_RANK, -1).T.astype(BF16)
    P["mla_w_out"] = mla_w_out[0].astype(BF16)
    P["ffn"] = [(ffn_w_gate[l].astype(BF16), ffn_w_up[l].astype(BF16),
                 ffn_w_down[l].astype(BF16)) for l in range(2)]
    return P


def _rope_slabs(S):
    inv_freq = 1.0 / (ROPE_THETA ** (jnp.arange(0, MLA_ROPE, 2, dtype=F32) / MLA_ROPE))
    ang = jnp.arange(S, dtype=F32)[:, None] * inv_freq[None, :]
    cos, sin = jnp.cos(ang), jnp.sin(ang)
    pad = LANES - MLA_QK
    cos_tab = jnp.concatenate([jnp.ones((S, MLA_NOPE), F32), cos, cos, jnp.zeros((S, pad), F32)], 1)
    sin_tab = jnp.concatenate([jnp.zeros((S, MLA_NOPE), F32), sin, sin, jnp.zeros((S, pad), F32)], 1)
    return cos_tab, sin_tab


def _trunk(x, P):
    S = x.shape[1]
    xg, qkv = _ab_in_proj(x, P["norm_mix"][0], P["ab_w_in"])
    hf = _rglru_pass(xg, None, P["conv_w"], P["conv_b"], P["rg_w"][0], P["rg_b"][0],
                     P["rg_lam"][0], reverse=False)
    y_rnn = _rglru_pass(xg, hf, P["conv_w"], P["conv_b"], P["rg_w"][1], P["rg_b"][1],
                        P["rg_lam"][1], reverse=True)
    o = _dilated_attention(qkv)
    x = _out_ffn(x, [y_rnn, o], P["ab_w_out"], P["norm_ffn"][0], *P["ffn"][0])
    cos_tab, sin_tab = _rope_slabs(S)
    q, k, vt = _mla_in_proj(x, P["norm_mix"][1], P["mla_w_in"], P["mla_q_norm"], P["mla_kv_norm"],
                            P["mla_wq"], P["mla_wk"], P["mla_wvt"], cos_tab, sin_tab)
    o = _mla_attention(q, k, vt)
    return _out_ffn(x, [o], P["mla_w_out"], P["norm_ffn"][1], *P["ffn"][1],
                    g_final=P["norm_final"])


def kernel(x_prompt, x_sample, norm_mix, norm_ffn, norm_final, ab_w_in, ab_conv_w, ab_conv_b, rg_w_a, rg_b_a, rg_w_i, rg_b_i, rg_lam, ab_w_out, mla_w_in, mla_q_norm, mla_w_qb, mla_kv_norm, mla_w_kvb, mla_w_out, ffn_w_gate, ffn_w_up, ffn_w_down):
    P = _prepare(norm_mix, norm_ffn, norm_final, ab_w_in, ab_conv_w, ab_conv_b, rg_w_a, rg_b_a,
                 rg_w_i, rg_b_i, rg_lam, ab_w_out, mla_w_in, mla_q_norm, mla_w_qb, mla_kv_norm,
                 mla_w_kvb, mla_w_out, ffn_w_gate, ffn_w_up, ffn_w_down)
    return (_trunk(x_prompt, P), _trunk(x_sample, P))
```

```python
import functools
import math

import jax
import jax.numpy as jnp
from jax import lax
from jax.experimental import pallas as pl
from jax.experimental.pallas import tpu as pltpu

F32 = jnp.float32
BF16 = jnp.bfloat16

D_MODEL = 1024
EPS = 1e-6
RG_WIDTH = 512
RG_BLOCKS = 8
RG_C = 8.0
DA_HEADS = 8
DA_HEAD_DIM = 64
DA_WIDTH = 512
DA_DILATIONS = (1, 4, 16)
DA_HALF_STEPS = 64
DA_TOKENS = DA_HALF_STEPS * max(DA_DILATIONS)
DA_Q_SLOT = 128
DA_K_SLOT = 256
MLA_HEADS = 16
MLA_Q_RANK = 384
MLA_KV_RANK = 256
MLA_NOPE = 64
MLA_ROPE = 32
MLA_V = 64
MLA_QK = MLA_NOPE + MLA_ROPE
MLA_ONES = 16
MLA_VROWS = MLA_V + MLA_ONES
MLA_MAX_LAG_EXCESS = 100.0
ROPE_THETA = 10000.0
FFN_HIDDEN = 2816
NEG_BIG = -1e30
LANES = 128

VMEM_LIMIT = 56 * 1024 * 1024

ROW_TILE = 512
SCAN_TILE = 512
SCAN_PAD = 8
MLA_Q_TILE = 2048
MLA_K_TILE = ROW_TILE


def _params(*sem):
    return pltpu.CompilerParams(dimension_semantics=sem, vmem_limit_bytes=VMEM_LIMIT)


def _const_spec(shape):
    nd = len(shape)
    return pl.BlockSpec(shape, lambda *_: (0,) * nd)


def _rms(x, g):
    return x * lax.rsqrt(jnp.mean(x * x, axis=-1, keepdims=True) + EPS) * g


def _ab_in_kernel(x_ref, g_ref, w_ref, xg_ref, qkv_ref):
    h = _rms(x_ref[...], g_ref[...]).astype(BF16)
    xg_ref[...] = jnp.dot(h, w_ref[:, :2 * RG_WIDTH], preferred_element_type=F32)
    n_slabs = 3 * DA_WIDTH // LANES
    qkv = jnp.dot(h, w_ref[:, 2 * RG_WIDTH:], preferred_element_type=F32)
    for j in range(n_slabs):
        qkv_ref[j] = qkv[:, LANES * j:LANES * (j + 1)]


def _ab_in_proj(x, g, w):
    B, S, D = x.shape
    n_slabs = 3 * DA_WIDTH // LANES
    tm = ROW_TILE
    return pl.pallas_call(
        _ab_in_kernel,
        grid=(B, S // tm),
        in_specs=[
            pl.BlockSpec((None, tm, D), lambda b, i: (b, i, 0)),
            _const_spec(g.shape),
            _const_spec(w.shape),
        ],
        out_specs=[
            pl.BlockSpec((None, tm, 2 * RG_WIDTH), lambda b, i: (b, i, 0)),
            pl.BlockSpec((None, n_slabs, tm, LANES), lambda b, i: (b, 0, i, 0)),
        ],
        out_shape=[
            jax.ShapeDtypeStruct((B, S, 2 * RG_WIDTH), F32),
            jax.ShapeDtypeStruct((B, n_slabs, S, LANES), F32),
        ],
        compiler_params=_params("parallel", "parallel"),
        name="ab_in_proj",
    )(x, g, w)


def _shift_rows(x, d, edge, toward_end):
    t = SCAN_PAD
    n = x.shape[0] // t
    sub = lax.broadcasted_iota(jnp.int32, (t, x.shape[1]), 0)
    tiles = [x[t * i:t * (i + 1)] for i in range(n)]
    if toward_end:
        rot = [pltpu.roll(b, d, 0) for b in [edge] + tiles]
        out = [jnp.where(sub < d, rot[i], rot[i + 1]) for i in range(n)]
    else:
        rot = [pltpu.roll(b, t - d, 0) for b in tiles + [edge]]
        out = [jnp.where(sub >= t - d, rot[i + 1], rot[i]) for i in range(n)]
    return jnp.concatenate(out, axis=0)


def _rglru_kernel(reverse, n_chunks, *refs):
    if reverse:
        (x_ref, xp_ref, xn_ref, gate_ref, hf_ref, cw_ref, cb_ref, wg_ref, bg_ref, lam_ref,
         out_ref, carry_ref) = refs
    else:
        (x_ref, xp_ref, xn_ref, cw_ref, cb_ref, wg_ref, bg_ref, lam_ref,
         out_ref, carry_ref) = refs
    i = pl.program_id(1)
    c = (n_chunks - 1 - i) if reverse else i
    tc, width = x_ref.shape

    @pl.when(i == 0)
    def _():
        carry_ref[...] = jnp.zeros_like(carry_ref)

    x = x_ref[...]
    prev = jnp.where(c > 0, xp_ref[...], 0.0)
    nxt = jnp.where(c < n_chunks - 1, xn_ref[...], 0.0)
    cw = cw_ref[...]
    xc = (cw[0:1] * _shift_rows(x, 2, prev, True) + cw[1:2] * _shift_rows(x, 1, prev, True)
          + cw[2:3] * x + cw[3:4] * _shift_rows(x, 1, nxt, False) + cb_ref[...])

    z = jnp.dot(xc.astype(BF16), wg_ref[...], preferred_element_type=F32) + bg_ref[...]
    gates = 0.5 * jnp.tanh(0.5 * z) + 0.5
    r = gates[:, :width]
    gi = gates[:, width:]
    log_a = (-RG_C) * r * jax.nn.softplus(-lam_ref[...])
    a = jnp.exp(log_a)
    z2 = -jnp.tanh(log_a) * (a * a + 1.0)
    u = jnp.where(z2 > 0.0, z2 * lax.rsqrt(z2), 0.0) * (gi * xc)

    one_edge = jnp.ones((SCAN_PAD, width), F32)
    zero_edge = jnp.zeros((SCAN_PAD, width), F32)
    d = 1
    while d < tc:
        if d < SCAN_PAD:
            a_sh = _shift_rows(a, d, one_edge, not reverse)
            u = a * _shift_rows(u, d, zero_edge, not reverse) + u
            a = a * a_sh
        elif reverse:
            u = jnp.concatenate([a[:tc - d] * u[d:] + u[:tc - d], u[tc - d:]], axis=0)
            a = jnp.concatenate([a[:tc - d] * a[d:], a[tc - d:]], axis=0)
        else:
            u = jnp.concatenate([u[:d], a[d:] * u[:tc - d] + u[d:]], axis=0)
            a = jnp.concatenate([a[:d], a[d:] * a[:tc - d]], axis=0)
        d *= 2
    h = u + a * carry_ref[...]
    carry_ref[...] = h[0:1] if reverse else h[tc - 1:tc]

    if reverse:
        out_ref[...] = (jax.nn.gelu(gate_ref[...]) * (hf_ref[...] + h)).astype(out_ref.dtype)
    else:
        out_ref[...] = h


def _rglru_pass(xg, hf, conv_w, conv_b, wg, bg, lam, reverse):
    B, S, _ = xg.shape
    tc = SCAN_TILE
    n_chunks = S // tc
    sub = SCAN_PAD

    def chunk(i):
        return (n_chunks - 1 - i) if reverse else i

    x_spec = pl.BlockSpec((None, tc, RG_WIDTH), lambda b, i: (b, chunk(i), 0))
    prev_spec = pl.BlockSpec(
        (None, sub, RG_WIDTH), lambda b, i: (b, jnp.maximum(chunk(i) * (tc // sub) - 1, 0), 0))
    next_spec = pl.BlockSpec(
        (None, sub, RG_WIDTH),
        lambda b, i: (b, jnp.minimum((chunk(i) + 1) * (tc // sub), S // sub - 1), 0))
    consts = [conv_w, conv_b, wg, bg, lam]
    const_specs = [_const_spec(t.shape) for t in consts]
    if reverse:
        gate_spec = pl.BlockSpec((None, tc, RG_WIDTH), lambda b, i: (b, chunk(i), 1))
        hf_spec = pl.BlockSpec((None, tc, RG_WIDTH), lambda b, i: (b, chunk(i), 0))
        inputs = [xg, xg, xg, xg, hf] + consts
        in_specs = [x_spec, prev_spec, next_spec, gate_spec, hf_spec] + const_specs
        out_dtype = BF16
    else:
        inputs = [xg, xg, xg] + consts
        in_specs = [x_spec, prev_spec, next_spec] + const_specs
        out_dtype = F32
    return pl.pallas_call(
        functools.partial(_rglru_kernel, reverse, n_chunks),
        grid=(B, n_chunks),
        in_specs=in_specs,
        out_specs=pl.BlockSpec((None, tc, RG_WIDTH), lambda b, i: (b, chunk(i), 0)),
        out_shape=jax.ShapeDtypeStruct((B, S, RG_WIDTH), out_dtype),
        scratch_shapes=[pltpu.VMEM((1, RG_WIDTH), F32)],
        compiler_params=_params("parallel", "arbitrary"),
        name="rglru_bwd" if reverse else "rglru_fwd",
    )(*inputs)


def _dilated_kernel(n_blocks, slopes, q_ref, kp_ref, kc_ref, kn_ref, vp_ref, vc_ref, vn_ref,
                    out_ref, qs_ref, ks_ref, vs_ref, o_scr, lse_scr, bias_scr):
    i = pl.program_id(1)
    T = DA_TOKENS
    W = DA_HALF_STEPS
    n_pairs = DA_WIDTH // LANES
    log2e = math.log2(math.e)
    scale = log2e / math.sqrt(DA_HEAD_DIM)
    is_first = i == 0
    is_last = i == n_blocks - 1

    for g, dil in enumerate(DA_DILATIONS):
        n = T // dil
        bq = min(n, 128)
        bk = bq + 2 * W
        rowq = lax.broadcasted_iota(jnp.int32, (bq, bk), 0)
        colk = lax.broadcasted_iota(jnp.int32, (bq, bk), 1)
        rel = colk - W - rowq
        band = jnp.abs(rel) <= W
        neg_dist = -(jnp.abs(rel) * dil).astype(F32)
        lane = lax.broadcasted_iota(jnp.int32, (bq, LANES), 1)
        lo_lane = lane < DA_HEAD_DIM
        for head in range(DA_HEADS):
            bias_scr[head, 0:bq, 0:bk] = jnp.where(band, (slopes[head] * log2e) * neg_dist,
                                                   NEG_BIG)

        def stage(r, slot, dil=dil, n=n):
            qo, ko = slot * DA_Q_SLOT, slot * DA_K_SLOT
            for p in range(n_pairs):
                qs_ref[p, qo:qo + n] = (q_ref[p, pl.ds(r, n, stride=dil), :] * scale).astype(BF16)
                for src_p, src_c, src_n, dst in ((kp_ref, kc_ref, kn_ref, ks_ref),
                                                 (vp_ref, vc_ref, vn_ref, vs_ref)):
                    dst[p, ko:ko + W] = src_p[p, pl.ds(r + dil * (n - W), W, stride=dil),
                                              :].astype(BF16)
                    dst[p, ko + W:ko + W + n] = src_c[p, pl.ds(r, n, stride=dil), :].astype(BF16)
                    dst[p, ko + W + n:ko + 2 * W + n] = src_n[p, pl.ds(r, W, stride=dil),
                                                             :].astype(BF16)

        def attend(items, dil=dil, n=n, bq=bq, bk=bk, g=g, colk=colk, lo_lane=lo_lane):
            def start(q0, slot, pitch):
                return q0 if slot == 0 else slot * pitch + q0

            scores = []
            for r, q0, slot in items:
                for p in range(n_pairs):
                    qp = qs_ref[p, pl.ds(start(q0, slot, DA_Q_SLOT), bq), :]
                    kp = ks_ref[p, pl.ds(start(q0, slot, DA_K_SLOT), bk), :]
                    for hh in range(2):
                        qh = jnp.where(lo_lane if hh == 0 else jnp.logical_not(lo_lane), qp,
                                       jnp.zeros_like(qp))
                        scores.append(lax.dot_general(qh, kp, (((1,), (1,)), ((), ())),
                                                      preferred_element_type=F32))
            probs, lse2, inv_l = [], [], []
            for idx, s in enumerate(scores):
                r, q0, slot = items[idx // DA_HEADS]
                pos = q0 + colk
                in_seq = jnp.logical_not(
                    jnp.logical_or(jnp.logical_and(is_first, pos < W),
                                   jnp.logical_and(is_last, pos >= n + W)))
                s = jnp.where(in_seq, s + bias_scr[idx % DA_HEADS, 0:bq, 0:bk], NEG_BIG)
                m = jnp.max(s, axis=-1, keepdims=True)
                e = jnp.exp2(s - m)
                l = jnp.sum(e, axis=-1, keepdims=True)
                probs.append(e.astype(BF16))
                inv_l.append(1.0 / l)
                lse2.append(m + jnp.log2(l))
            for it, (r, q0, slot) in enumerate(items):
                for p in range(n_pairs):
                    vp = vs_ref[p, pl.ds(start(q0, slot, DA_K_SLOT), bk), :]
                    h0 = it * DA_HEADS + 2 * p
                    o_h = [jnp.dot(probs[h0 + hh], vp, preferred_element_type=F32)
                           * inv_l[h0 + hh] for hh in range(2)]
                    rows = pl.ds(r + dil * q0, bq, stride=dil)
                    o_scr[g, p, rows, :] = jnp.where(lo_lane, o_h[0], o_h[1])
                    lse_scr[g, p, rows, :] = jnp.where(lo_lane, lse2[h0], lse2[h0 + 1])

        if n == bq:
            def class_pair(t, carry, stage=stage, attend=attend):
                stage(2 * t, 0)
                stage(2 * t + 1, 1)
                attend([(2 * t, 0, 0), (2 * t + 1, 0, 1)])
                return carry

            lax.fori_loop(0, dil // 2, class_pair, 0)
        else:
            def class_body(r, carry, stage=stage, attend=attend, bq=bq, n=n):
                stage(r, 0)

                def qb_pair(t, carry2):
                    q0 = pl.multiple_of(t * (2 * bq), 2 * bq)
                    attend([(r, q0, 0), (r, pl.multiple_of(q0 + bq, bq), 0)])
                    return carry2

                lax.fori_loop(0, n // (2 * bq), qb_pair, 0)
                return carry

            lax.fori_loop(0, dil, class_body, 0)

    rows_per_step = 256

    def merge_body(t, carry):
        t0 = pl.multiple_of(t * rows_per_step, rows_per_step)
        rows = pl.ds(t0, rows_per_step)
        for p in range(n_pairs):
            lse = [lse_scr[g, p, rows, :] for g in range(len(DA_DILATIONS))]
            mx = jnp.maximum(jnp.maximum(lse[0], lse[1]), lse[2])
            w = [jnp.exp2(v - mx) for v in lse]
            num = sum(w[g] * o_scr[g, p, rows, :] for g in range(len(DA_DILATIONS)))
            out_ref[rows, LANES * p:LANES * (p + 1)] = (num / (w[0] + w[1] + w[2])).astype(
                out_ref.dtype)
        return carry

    lax.fori_loop(0, T // rows_per_step, merge_body, 0)


def _dilated_attention(qkv):
    B, _, S, _ = qkv.shape
    T = DA_TOKENS
    n_blocks = S // T
    n_pairs = DA_WIDTH // LANES
    slopes = tuple(2.0 ** (-8.0 * (h + 1) / DA_HEADS) for h in range(DA_HEADS))

    def spec(slab, shift):
        def index(b, i):
            return (b, slab, jnp.clip(i + shift, 0, n_blocks - 1), 0)
        return pl.BlockSpec((None, n_pairs, T, LANES), index)

    win = T + 2 * DA_HALF_STEPS
    return pl.pallas_call(
        functools.partial(_dilated_kernel, n_blocks, slopes),
        grid=(B, n_blocks),
        in_specs=[spec(0, 0), spec(1, -1), spec(1, 0), spec(1, 1),
                  spec(2, -1), spec(2, 0), spec(2, 1)],
        out_specs=pl.BlockSpec((None, T, DA_WIDTH), lambda b, i: (b, i, 0)),
        out_shape=jax.ShapeDtypeStruct((B, S, DA_WIDTH), BF16),
        scratch_shapes=[
            pltpu.VMEM((n_pairs, T, LANES), BF16),
            pltpu.VMEM((n_pairs, win, LANES), BF16),
            pltpu.VMEM((n_pairs, win, LANES), BF16),
            pltpu.VMEM((len(DA_DILATIONS), n_pairs, T, LANES), F32),
            pltpu.VMEM((len(DA_DILATIONS), n_pairs, T, LANES), F32),
            pltpu.VMEM((DA_HEADS, 128, 128 + 2 * DA_HALF_STEPS), F32),
        ],
        compiler_params=_params("parallel", "parallel"),
        name="dilated_attention",
    )(qkv, qkv, qkv, qkv, qkv, qkv, qkv)


def _out_ffn_kernel(n_y, final_norm, *refs):
    x_ref = refs[0]
    y_refs = refs[1:1 + n_y]
    wo_ref, g_ref, wg_ref, wu_ref, wd_ref = refs[1 + n_y:6 + n_y]
    gf_ref = refs[6 + n_y] if final_norm else None
    out_ref = refs[-1]

    y = y_refs[0][...] if n_y == 1 else jnp.concatenate([r[...] for r in y_refs], axis=1)
    x = x_ref[...] + jnp.dot(y, wo_ref[...], preferred_element_type=F32)
    h = _rms(x, g_ref[...]).astype(BF16)
    gate = jnp.dot(h, wg_ref[...], preferred_element_type=F32)
    up = jnp.dot(h, wu_ref[...], preferred_element_type=F32)
    act = (jax.nn.silu(gate) * up).astype(BF16)
    x = x + jnp.dot(act, wd_ref[...], preferred_element_type=F32)
    if final_norm:
        x = _rms(x, gf_ref[...])
    out_ref[...] = x


def _out_ffn(x, ys, w_out, g, wg, wu, wd, g_final=None):
    B, S, D = x.shape
    rows = B * S
    tm = ROW_TILE
    x2 = x.reshape(rows, D)
    ys2 = [y.reshape(rows, y.shape[-1]) for y in ys]
    consts = [w_out, g, wg, wu, wd] + ([g_final] if g_final is not None else [])
    out = pl.pallas_call(
        functools.partial(_out_ffn_kernel, len(ys), g_final is not None),
        grid=(rows // tm,),
        in_specs=([pl.BlockSpec((tm, D), lambda i: (i, 0))]
                  + [pl.BlockSpec((tm, y.shape[-1]), lambda i: (i, 0)) for y in ys2]
                  + [pl.BlockSpec(t.shape, lambda i, nd=t.ndim: (0,) * nd,
                                  pipeline_mode=pl.Buffered(1)) for t in consts]),
        out_specs=pl.BlockSpec((tm, D), lambda i: (i, 0)),
        out_shape=jax.ShapeDtypeStruct((rows, D), F32),
        compiler_params=pltpu.CompilerParams(
            dimension_semantics=("parallel",), vmem_limit_bytes=VMEM_LIMIT,
            allow_input_fusion=[False] * (1 + len(ys2)) + [t.ndim == 2 and t.shape[0] > 1
                                                          for t in consts]),
        name="out_proj_ffn",
    )(x2, *ys2, *consts)
    return out.reshape(B, S, D)


def _mla_in_kernel(x_ref, g_ref, win_ref, qn_ref, kvn_ref, wq_ref, wk_ref, wvt_ref, cos_ref,
                   sin_ref, q_ref, k_ref, vt_ref):
    h = _rms(x_ref[...], g_ref[...]).astype(BF16)
    proj = jnp.dot(h, win_ref[...], preferred_element_type=F32)
    cq = _rms(proj[:, :MLA_Q_RANK], qn_ref[...]).astype(BF16)
    ckv_f32 = _rms(proj[:, MLA_Q_RANK:MLA_Q_RANK + MLA_KV_RANK], kvn_ref[...])
    ckv = ckv_f32.astype(BF16)
    base = MLA_Q_RANK + MLA_KV_RANK
    cos = cos_ref[...]
    sin = sin_ref[...]
    k_rope = proj[:, base:base + LANES] * cos + proj[:, base + LANES:base + 2 * LANES] * sin
    scale = math.log2(math.e) / math.sqrt(MLA_QK)
    for hd in range(MLA_HEADS):
        cols = slice(hd * LANES, (hd + 1) * LANES)
        qq = jnp.dot(cq, wq_ref[:, 2 * hd * LANES:2 * (hd + 1) * LANES],
                     preferred_element_type=F32)
        q_ref[:, cols] = ((qq[:, :LANES] * cos + qq[:, LANES:] * sin) * scale).astype(BF16)
    for pair in range(MLA_HEADS // 2):
        cols = slice(2 * pair * LANES, 2 * (pair + 1) * LANES)
        kn = jnp.dot(ckv, wk_ref[:, cols], preferred_element_type=F32)
        k_ref[:, cols] = (kn + jnp.concatenate([k_rope, k_rope], axis=1)).astype(BF16)
    vt = jnp.dot(wvt_ref[...], ckv_f32.T.astype(BF16), preferred_element_type=F32)
    ones = jnp.ones((MLA_ONES, vt.shape[1]), BF16)
    for hd in range(MLA_HEADS):
        vt_ref[hd * MLA_VROWS:hd * MLA_VROWS + MLA_V, :] = (
            vt[hd * MLA_V:(hd + 1) * MLA_V].astype(BF16))
        vt_ref[hd * MLA_VROWS + MLA_V:(hd + 1) * MLA_VROWS, :] = ones


def _mla_in_proj(x, g, win, qn, kvn, wq, wk, wvt, cos_tab, sin_tab):
    B, S, D = x.shape
    tm = ROW_TILE
    width = MLA_HEADS * LANES
    consts = [g, win, qn, kvn, wq, wk, wvt]
    return pl.pallas_call(
        _mla_in_kernel,
        grid=(B, S // tm),
        in_specs=([pl.BlockSpec((None, tm, D), lambda b, i: (b, i, 0))]
                  + [_const_spec(t.shape) for t in consts]
                  + [pl.BlockSpec((tm, LANES), lambda b, i: (i, 0)),
                     pl.BlockSpec((tm, LANES), lambda b, i: (i, 0))]),
        out_specs=[
            pl.BlockSpec((None, tm, width), lambda b, i: (b, i, 0)),
            pl.BlockSpec((None, tm, width), lambda b, i: (b, i, 0)),
            pl.BlockSpec((None, None, MLA_HEADS * MLA_VROWS, tm), lambda b, i: (b, i, 0, 0)),
        ],
        out_shape=[
            jax.ShapeDtypeStruct((B, S, width), BF16),
            jax.ShapeDtypeStruct((B, S, width), BF16),
            jax.ShapeDtypeStruct((B, S // tm, MLA_HEADS * MLA_VROWS, tm), BF16),
        ],
        compiler_params=_params("parallel", "parallel"),
        name="mla_in_proj",
    )(x, *consts, cos_tab, sin_tab)


def _mla_attn_kernel(q_ref, k_ref, vt_ref, o_ref, acc_scr):
    n_kb, _, bk = vt_ref.shape
    bq = q_ref.shape[0]
    qs = [q_ref[:, hh * LANES:(hh + 1) * LANES] for hh in range(2)]
    rows = vt_ref.shape[1] // 2
    nt = (((1,), (1,)), ((), ()))

    def scores(j, hh):
        return lax.dot_general(k_ref[j * bk:(j + 1) * bk, hh * LANES:(hh + 1) * LANES], qs[hh],
                               nt, preferred_element_type=F32)

    def write_output():
        outs = [acc_scr[hh, 0:MLA_V] / acc_scr[hh, MLA_V:MLA_V + 1] for hh in range(2)]
        o_ref[...] = jnp.concatenate(outs, axis=0).T.astype(o_ref.dtype)

    ref_max = [None, None]
    run_max = [None, None]
    excess = jnp.zeros((1, bq), F32)
    sts = [scores(0, hh) for hh in range(2)]
    for j in range(n_kb):
        for hh in range(2):
            blk_max = jnp.max(sts[hh], axis=0, keepdims=True)
            if j == 0:
                new_ref, alpha = blk_max, None
                run_max[hh] = blk_max
            else:
                new_ref = run_max[hh]
                alpha = jnp.exp2(ref_max[hh] - new_ref)
                excess = jnp.maximum(excess, blk_max - new_ref)
                run_max[hh] = jnp.maximum(run_max[hh], blk_max)
            ref_max[hh] = new_ref
            p = jnp.exp2(sts[hh] - new_ref).astype(BF16)
            if j + 1 < n_kb:
                sts[hh] = scores(j + 1, hh)
            pv = jnp.dot(vt_ref[j, hh * rows:(hh + 1) * rows, :], p, preferred_element_type=F32)
            acc_scr[hh] = pv if alpha is None else alpha * acc_scr[hh] + pv
    write_output()

    @pl.when(jnp.max(excess) > MLA_MAX_LAG_EXCESS)
    def _():
        acc_scr[...] = jnp.zeros_like(acc_scr)

        def body(j, ms):
            k0 = pl.multiple_of(j * bk, bk)
            new = []
            for hh in range(2):
                st = lax.dot_general(k_ref[pl.ds(k0, bk), hh * LANES:(hh + 1) * LANES], qs[hh],
                                     nt, preferred_element_type=F32)
                m_new = jnp.maximum(ms[hh], jnp.max(st, axis=0, keepdims=True))
                p = jnp.exp2(st - m_new).astype(BF16)
                pv = jnp.dot(vt_ref[j, hh * rows:(hh + 1) * rows, :], p,
                             preferred_element_type=F32)
                acc_scr[hh] = jnp.exp2(ms[hh] - m_new) * acc_scr[hh] + pv
                new.append(m_new)
            return tuple(new)

        lax.fori_loop(0, n_kb, body, tuple(jnp.full((1, bq), NEG_BIG, F32) for _ in range(2)))
        write_output()


def _mla_attention(q, k, vt):
    B, S, _ = q.shape
    n_kb = vt.shape[1]
    bq = MLA_Q_TILE
    n_pairs = MLA_HEADS // 2
    return pl.pallas_call(
        _mla_attn_kernel,
        grid=(B, n_pairs, S // bq),
        in_specs=[
            pl.BlockSpec((None, bq, 2 * LANES), lambda b, p, i: (b, i, p)),
            pl.BlockSpec((None, S, 2 * LANES), lambda b, p, i: (b, 0, p)),
            pl.BlockSpec((None, n_kb, 2 * MLA_VROWS, MLA_K_TILE), lambda b, p, i: (b, 0, p, 0)),
        ],
        out_specs=pl.BlockSpec((None, bq, 2 * MLA_V), lambda b, p, i: (b, i, p)),
        out_shape=jax.ShapeDtypeStruct((B, S, MLA_HEADS * MLA_V), BF16),
        scratch_shapes=[
            pltpu.VMEM((2, MLA_VROWS, bq), F32),
        ],
        compiler_params=_params("parallel", "parallel", "arbitrary"),
        name="mla_attention",
    )(q, k, vt)


def _block_diag(w):
    nb, d, _ = w.shape
    eye = jnp.eye(nb, dtype=w.dtype)
    return (eye[:, None, :, None] * w[:, :, None, :]).reshape(nb * d, nb * d)


def _rot_cols(w):
    half = MLA_ROPE // 2
    return jnp.concatenate([-w[..., half:], w[..., :half]], axis=-1)


def _pad_lanes(w, offset):
    return jnp.pad(w, [(0, 0)] * (w.ndim - 1) + [(offset, LANES - offset - w.shape[-1])])


def _prepare(norm_mix, norm_ffn, norm_final, ab_w_in, ab_conv_w, ab_conv_b, rg_w_a, rg_b_a,
             rg_w_i, rg_b_i, rg_lam, ab_w_out, mla_w_in, mla_q_norm, mla_w_qb, mla_kv_norm,
             mla_w_kvb, mla_w_out, ffn_w_gate, ffn_w_up, ffn_w_down):
    row = lambda v: v.reshape(1, -1).astype(F32)
    P = {}
    P["norm_mix"] = [row(norm_mix[l]) for l in range(2)]
    P["norm_ffn"] = [row(norm_ffn[l]) for l in range(2)]
    P["norm_final"] = row(norm_final)
    P["ab_w_in"] = ab_w_in[0].astype(BF16)
    P["conv_w"] = ab_conv_w[0].astype(F32)
    P["conv_b"] = row(ab_conv_b[0])
    P["rg_w"] = [jnp.concatenate([_block_diag(rg_w_a[0, d]), _block_diag(rg_w_i[0, d])],
                                 axis=1).astype(BF16) for d in range(2)]
    P["rg_b"] = [jnp.concatenate([rg_b_a[0, d], rg_b_i[0, d]]).reshape(1, -1) for d in range(2)]
    P["rg_lam"] = [row(rg_lam[0, d]) for d in range(2)]
    P["ab_w_out"] = ab_w_out[0].astype(BF16)

    w_in = mla_w_in[0]
    base = MLA_Q_RANK + MLA_KV_RANK
    w_kr = w_in[:, base:]
    P["mla_w_in"] = jnp.concatenate(
        [w_in[:, :base], _pad_lanes(w_kr, MLA_NOPE), _pad_lanes(_rot_cols(w_kr), MLA_NOPE)],
        axis=1).astype(BF16)
    P["mla_q_norm"] = row(mla_q_norm[0])
    P["mla_kv_norm"] = row(mla_kv_norm[0])
    wq = mla_w_qb[0].reshape(MLA_Q_RANK, MLA_HEADS, MLA_QK)
    wq_plain = _pad_lanes(wq, 0)
    wq_rot = _pad_lanes(_rot_cols(wq[..., MLA_NOPE:]), MLA_NOPE)
    P["mla_wq"] = jnp.concatenate([wq_plain, wq_rot], axis=-1).reshape(MLA_Q_RANK, -1).astype(BF16)
    wkv = mla_w_kvb[0].reshape(MLA_KV_RANK, MLA_HEADS, MLA_NOPE + MLA_V)
    P["mla_wk"] = _pad_lanes(wkv[..., :MLA_NOPE], 0).reshape(MLA_KV_RANK, -1).astype(BF16)
    P["mla_wvt"] = wkv[..., MLA_NOPE:].reshape(MLA_KV_RANK, -1).T.astype(BF16)
    P["mla_w_out"] = mla_w_out[0].astype(BF16)
    P["ffn"] = [(ffn_w_gate[l].astype(BF16), ffn_w_up[l].astype(BF16),
                 ffn_w_down[l].astype(BF16)) for l in range(2)]
    return P


def _rope_slabs(S):
    inv_freq = 1.0 / (ROPE_THETA ** (jnp.arange(0, MLA_ROPE, 2, dtype=F32) / MLA_ROPE))
    ang = jnp.arange(S, dtype=F32)[:, None] * inv_freq[None, :]
    cos, sin = jnp.cos(ang), jnp.sin(ang)
    pad = LANES - MLA_QK
    cos_tab = jnp.concatenate([jnp.ones((S, MLA_NOPE), F32), cos, cos, jnp.zeros((S, pad), F32)], 1)
    sin_tab = jnp.concatenate([jnp.zeros((S, MLA_NOPE), F32), sin, sin, jnp.zeros((S, pad), F32)], 1)
    return cos_tab, sin_tab


def _trunk(x, P):
    S = x.shape[1]
    xg, qkv = _ab_in_proj(x, P["norm_mix"][0], P["ab_w_in"])
    hf = _rglru_pass(xg, None, P["conv_w"], P["conv_b"], P["rg_w"][0], P["rg_b"][0],
                     P["rg_lam"][0], reverse=False)
    y_rnn = _rglru_pass(xg, hf, P["conv_w"], P["conv_b"], P["rg_w"][1], P["rg_b"][1],
                        P["rg_lam"][1], reverse=True)
    o = _dilated_attention(qkv)
    x = _out_ffn(x, [y_rnn, o], P["ab_w_out"], P["norm_ffn"][0], *P["ffn"][0])
    cos_tab, sin_tab = _rope_slabs(S)
    q, k, vt = _mla_in_proj(x, P["norm_mix"][1], P["mla_w_in"], P["mla_q_norm"], P["mla_kv_norm"],
                            P["mla_wq"], P["mla_wk"], P["mla_wvt"], cos_tab, sin_tab)
    o = _mla_attention(q, k, vt)
    return _out_ffn(x, [o], P["mla_w_out"], P["norm_ffn"][1], *P["ffn"][1],
                    g_final=P["norm_final"])


def kernel(x_prompt, x_sample, norm_mix, norm_ffn, norm_final, ab_w_in, ab_conv_w, ab_conv_b, rg_w_a, rg_b_a, rg_w_i, rg_b_i, rg_lam, ab_w_out, mla_w_in, mla_q_norm, mla_w_qb, mla_kv_norm, mla_w_kvb, mla_w_out, ffn_w_gate, ffn_w_up, ffn_w_down):
    P = _prepare(norm_mix, norm_ffn, norm_final, ab_w_in, ab_conv_w, ab_conv_b, rg_w_a, rg_b_a,
                 rg_w_i, rg_b_i, rg_lam, ab_w_out, mla_w_in, mla_q_norm, mla_w_qb, mla_kv_norm,
                 mla_w_kvb, mla_w_out, ffn_w_gate, ffn_w_up, ffn_w_down)
    return (_trunk(x_prompt, P), _trunk(x_sample, P))
```
